```python
import jax, jax.numpy as jnp
from jax import lax
import numpy as np

D_MODEL = 1024
BATCH = 1
SEQ = 16384
DEPTH = 1

GM_WIDTH = D_MODEL
GM_GROUPS = 8
GM_GROUP_DIM = GM_WIDTH // GM_GROUPS
GM_CHUNK = 128
N_HEADS = 16
HEAD_DIM = 64
ATT_WIDTH = N_HEADS * HEAD_DIM
MOBA_BLOCK = 256
MOBA_TOPK = 3
Q_BLOCK = 64
ROPE_THETA = 500000.0
ROT_DIM = HEAD_DIM // 4
D_FF = 2816
LN_EPS = 1e-5
DEEPNORM_ALPHA = (2.0 * DEPTH) ** 0.25
DEEPNORM_BETA = (8.0 * DEPTH) ** -0.25
SPLITS = (GM_WIDTH, GM_WIDTH, ATT_WIDTH, ATT_WIDTH, ATT_WIDTH, D_MODEL, D_MODEL)
IN_COLS = sum(SPLITS)

kernel_name = "hybrid_gmlp_moba_macaron_deepnorm"


def layer_norm(x, g, b):
    xf = x.astype(jnp.float32)
    mu = jnp.mean(xf, axis=-1, keepdims=True)
    var = jnp.mean(jnp.square(xf - mu), axis=-1, keepdims=True)
    y = (xf - mu) * lax.rsqrt(var + LN_EPS)
    return (y * g.astype(jnp.float32) + b.astype(jnp.float32)).astype(x.dtype)


def swiglu(x, w_gate, w_up, w_down):
    return (jax.nn.silu(x @ w_gate) * (x @ w_up)) @ w_down


def partial_rotary(x, pos):
    inv_freq = ROPE_THETA ** (-jnp.arange(0, ROT_DIM, 2, dtype=jnp.float32) / ROT_DIM)
    ang = pos.astype(jnp.float32)[:, None] * inv_freq[None, :]
    cos = jnp.cos(ang)[None, :, None, :]
    sin = jnp.sin(ang)[None, :, None, :]
    xr = x[..., :ROT_DIM].astype(jnp.float32)
    x1, x2 = xr[..., : ROT_DIM // 2], xr[..., ROT_DIM // 2:]
    rot = jnp.concatenate([x1 * cos - x2 * sin, x2 * cos + x1 * sin], axis=-1)
    return jnp.concatenate([rot.astype(x.dtype), x[..., ROT_DIM:]], axis=-1)


def gmlp_spatial_gating(u, v, ln_g, ln_b, w_s, b_s):
    B, S, _ = v.shape
    nc = S // GM_CHUNK
    vn = layer_norm(v, ln_g, ln_b).reshape(B, nc, GM_CHUNK, GM_GROUPS, GM_GROUP_DIM)
    causal = jnp.tril(jnp.ones((GM_CHUNK, GM_CHUNK), dtype=bool))
    ws = jnp.where(causal[None], w_s, 0)
    sv = jnp.einsum('gts,bnsgc->bntgc', ws, vn) + b_s.T[None, None, :, :, None]
    return u * sv.reshape(B, S, GM_WIDTH)


def moba_attention(q, k, v):
    B, S, H, dh = q.shape
    sp = -(-S // MOBA_BLOCK) * MOBA_BLOCK
    pad = ((0, 0), (0, sp - S), (0, 0), (0, 0))
    q, k, v = (jnp.pad(t, pad).transpose(0, 2, 1, 3) for t in (q, k, v))
    nb = sp // MOBA_BLOCK
    topk = min(MOBA_TOPK, nb - 1)
    kb = k.reshape(B, H, nb, MOBA_BLOCK, dh)
    vb = v.reshape(B, H, nb, MOBA_BLOCK, dh)
    kmean = jnp.mean(kb.astype(jnp.float32), axis=3)
    scale = dh ** -0.5
    nqb = sp // Q_BLOCK
    qblocks = q.reshape(B, H, nqb, Q_BLOCK, dh).transpose(2, 0, 1, 3, 4)
    bi = jnp.arange(B)[:, None, None, None]
    hi = jnp.arange(H)[None, :, None, None]

    def step(args):
        qi, qb_idx = args
        q_start = qb_idx * Q_BLOCK
        own = q_start // MOBA_BLOCK
        qpos = q_start + jnp.arange(Q_BLOCK)
        kpos = own * MOBA_BLOCK + jnp.arange(MOBA_BLOCK)
        k_own = lax.dynamic_index_in_dim(kb, own, axis=2, keepdims=False)
        v_own = lax.dynamic_index_in_dim(vb, own, axis=2, keepdims=False)
        s_own = jnp.einsum('bhqd,bhkd->bhqk', qi, k_own,
                           preferred_element_type=jnp.float32) * scale
        s_own = jnp.where(kpos[None, :] <= qpos[:, None], s_own, -jnp.inf)
        if topk > 0:
            gate = jnp.einsum('bhqd,bhnd->bhqn', qi.astype(jnp.float32), kmean)
            gate = jnp.where(jnp.arange(nb) < own, gate, -jnp.inf)
            _, sel = lax.top_k(gate, topk)
            valid = sel < own
            k_sel = kb[bi, hi, sel]
            v_sel = vb[bi, hi, sel]
            s_sel = jnp.einsum('bhqd,bhqjkd->bhqjk', qi, k_sel,
                               preferred_element_type=jnp.float32) * scale
            s_sel = jnp.where(valid[..., None], s_sel, -jnp.inf)
            s_all = jnp.concatenate([s_sel.reshape(B, H, Q_BLOCK, topk * MOBA_BLOCK), s_own], axis=-1)
            p = jax.nn.softmax(s_all, axis=-1).astype(v.dtype)
            p_sel = p[..., : topk * MOBA_BLOCK].reshape(B, H, Q_BLOCK, topk, MOBA_BLOCK)
            p_own = p[..., topk * MOBA_BLOCK:]
            out = (jnp.einsum('bhqjk,bhqjkd->bhqd', p_sel, v_sel)
                   + jnp.einsum('bhqk,bhkd->bhqd', p_own, v_own))
        else:
            p_own = jax.nn.softmax(s_own, axis=-1).astype(v.dtype)
            out = jnp.einsum('bhqk,bhkd->bhqd', p_own, v_own)
        return out

    out = lax.map(step, (qblocks, jnp.arange(nqb)))
    out = out.transpose(1, 0, 3, 2, 4).reshape(B, sp, H, dh)
    return out[:, :S]


def setup_inputs(seed: int = 0) -> dict:
    key = jax.random.key(seed)
    ks = jax.random.split(key, 24)
    L = DEPTH
    f32 = jnp.float32

    def nrm(k, shape, scale):
        return jax.random.normal(k, shape, f32) * scale

    def gain(k, shape):
        return 1.0 + 0.02 * jax.random.normal(k, shape, f32)

    return {
        "x": jax.random.normal(ks[0], (BATCH, SEQ, D_MODEL), f32),
        "ffn1_w_gate": nrm(ks[1], (L, D_MODEL, D_FF), D_MODEL ** -0.5),
        "ffn1_w_up": nrm(ks[2], (L, D_MODEL, D_FF), D_MODEL ** -0.5),
        "ffn1_w_down": nrm(ks[3], (L, D_FF, D_MODEL), DEEPNORM_BETA * D_FF ** -0.5),
        "ln1_g": gain(ks[4], (L, D_MODEL)),
        "ln1_b": nrm(ks[5], (L, D_MODEL), 0.02),
        "w_in": nrm(ks[6], (L, D_MODEL, IN_COLS), D_MODEL ** -0.5),
        "gm_ln_g": gain(ks[7], (L, GM_WIDTH)),
        "gm_ln_b": nrm(ks[8], (L, GM_WIDTH), 0.02),
        "gm_w_s": nrm(ks[9], (L, GM_GROUPS, GM_CHUNK, GM_CHUNK), GM_CHUNK ** -0.5),
        "gm_b_s": gain(ks[10], (L, GM_GROUPS, GM_CHUNK)),
        "w_gm_out": nrm(ks[11], (L, GM_WIDTH, D_MODEL), DEEPNORM_BETA * GM_WIDTH ** -0.5),
        "w_att_out": nrm(ks[12], (L, ATT_WIDTH, D_MODEL), DEEPNORM_BETA * ATT_WIDTH ** -0.5),
        "w_o": nrm(ks[13], (L, D_MODEL, D_MODEL), DEEPNORM_BETA * D_MODEL ** -0.5),
        "ln2_g": gain(ks[14], (L, D_MODEL)),
        "ln2_b": nrm(ks[15], (L, D_MODEL), 0.02),
        "ffn2_w_gate": nrm(ks[16], (L, D_MODEL, D_FF), D_MODEL ** -0.5),
        "ffn2_w_up": nrm(ks[17], (L, D_MODEL, D_FF), D_MODEL ** -0.5),
        "ffn2_w_down": nrm(ks[18], (L, D_FF, D_MODEL), DEEPNORM_BETA * D_FF ** -0.5),
        "ln3_g": gain(ks[19], (L, D_MODEL)),
        "ln3_b": nrm(ks[20], (L, D_MODEL), 0.02),
    }


def reference(x, ffn1_w_gate, ffn1_w_up, ffn1_w_down, ln1_g, ln1_b, w_in,
              gm_ln_g, gm_ln_b, gm_w_s, gm_b_s, w_gm_out, w_att_out, w_o,
              ln2_g, ln2_b, ffn2_w_gate, ffn2_w_up, ffn2_w_down, ln3_g, ln3_b):
    B, S, _ = x.shape
    pos = jnp.arange(S, dtype=jnp.int32)
    offs = np.cumsum(SPLITS)[:-1].tolist()
    for l in range(DEPTH):
        x = layer_norm(DEEPNORM_ALPHA * x + 0.5 * swiglu(x, ffn1_w_gate[l], ffn1_w_up[l], ffn1_w_down[l]),
                       ln1_g[l], ln1_b[l])
        proj = x @ w_in[l]
        u, v_gm, q, k, v_att, g_gm, g_att = jnp.split(proj, offs, axis=-1)
        y_gm = gmlp_spatial_gating(jax.nn.gelu(u), jax.nn.gelu(v_gm), gm_ln_g[l], gm_ln_b[l],
                                   gm_w_s[l], gm_b_s[l]) @ w_gm_out[l]
        q = partial_rotary(q.reshape(B, S, N_HEADS, HEAD_DIM), pos)
        k = partial_rotary(k.reshape(B, S, N_HEADS, HEAD_DIM), pos)
        v_att = v_att.reshape(B, S, N_HEADS, HEAD_DIM)
        y_att = moba_attention(q, k, v_att).reshape(B, S, ATT_WIDTH) @ w_att_out[l]
        merged = jax.nn.sigmoid(g_gm) * y_gm + jax.nn.sigmoid(g_att) * y_att
        x = layer_norm(DEEPNORM_ALPHA * x + merged @ w_o[l], ln2_g[l], ln2_b[l])
        x = layer_norm(DEEPNORM_ALPHA * x + 0.5 * swiglu(x, ffn2_w_gate[l], ffn2_w_up[l], ffn2_w_down[l]),
                       ln3_g[l], ln3_b[l])
    return x
```

```python
import functools

import jax
import jax.numpy as jnp
import numpy as np
from jax import lax
from jax.experimental import pallas as pl
from jax.experimental.pallas import tpu as pltpu

D_MODEL = 1024
DEPTH = 1
GM_GROUPS = 8
GM_CHUNK = 128
GM_GROUP_DIM = D_MODEL // GM_GROUPS
N_HEADS = 16
HEAD_DIM = 64
MOBA_BLOCK = 256
MOBA_TOPK = 3
ROPE_THETA = 500000.0
ROT_DIM = HEAD_DIM // 4
D_FF = 2816
LN_EPS = 1e-5
DEEPNORM_ALPHA = (2.0 * DEPTH) ** 0.25
ATT_SCALE = HEAD_DIM ** -0.5

LANES = 128
FFN_ROWS = 512
FFN_CHUNKS = ((0, 1024), (1024, 1024), (2048, 768))
PROJ_ROWS = 256
HEADS_PER_STEP = LANES // HEAD_DIM
VMEM_LIMIT = 56 * 1024 * 1024
NEG_BIG = -1e30

_BF16 = jnp.bfloat16
_F32 = jnp.float32


def _resident(shape):
    return pl.BlockSpec(shape, lambda *_: (0,) * len(shape), pipeline_mode=pl.Buffered(1))


def _layer_norm(z, g, b):
    mu = jnp.mean(z, axis=-1, keepdims=True)
    zc = z - mu
    var = jnp.mean(zc * zc, axis=-1, keepdims=True)
    return zc * lax.rsqrt(var + LN_EPS) * g + b


def _dot(a, b):
    return jnp.dot(a, b, preferred_element_type=_F32)


def _ffn_ln(x, wg_ref, wu_ref, wd_ref, g, b):
    xb = x.astype(_BF16)
    y = None
    for start, width in FFN_CHUNKS:
        gate = _dot(xb, wg_ref[:, start:start + width])
        up = _dot(xb, wu_ref[:, start:start + width])
        h = (jax.nn.silu(gate) * up).astype(_BF16)
        part = _dot(h, wd_ref[start:start + width, :])
        y = part if y is None else y + part
    return _layer_norm(DEEPNORM_ALPHA * x + 0.5 * y, g, b)


def _ffn1_kernel(x_ref, wg_ref, wu_ref, wd_ref, g_ref, b_ref, o_ref):
    o_ref[...] = _ffn_ln(x_ref[...], wg_ref, wu_ref, wd_ref, g_ref[...], b_ref[...])


def _ffn1_call(x, wg, wu, wd, g, b):
    S = x.shape[0]
    row = pl.BlockSpec((FFN_ROWS, D_MODEL), lambda i: (i, 0))
    return pl.pallas_call(
        _ffn1_kernel,
        grid=(S // FFN_ROWS,),
        in_specs=[row, _resident(wg.shape), _resident(wu.shape), _resident(wd.shape),
                  _resident(g.shape), _resident(b.shape)],
        out_specs=row,
        out_shape=jax.ShapeDtypeStruct((S, D_MODEL), _F32),
        compiler_params=pltpu.CompilerParams(
            dimension_semantics=("arbitrary",), vmem_limit_bytes=VMEM_LIMIT),
        name="ffn1_ln1",
    )(x, wg, wu, wd, g, b)


def _rotary(t, cos_t, sin_lo, sin_hi):
    half = ROT_DIM // 2
    cols = []
    for j in range(D_MODEL // LANES):
        tj = t[:, j * LANES:(j + 1) * LANES]
        fwd = pltpu.roll(tj, LANES - half, axis=1)
        bwd = pltpu.roll(tj, half, axis=1)
        cols.append(tj * cos_t + fwd * sin_lo + bwd * sin_hi)
    return jnp.concatenate(cols, axis=1)


def _proj_kernel(x_ref, win_ref, lng_ref, lnb_ref, ws_ref, bs_ref, wgo_ref,
                 cos_ref, slo_ref, shi_ref,
                 yg_ref, q_ref, k_ref, vt_ref, kmean_ref, sga_ref):
    xb = x_ref[...].astype(_BF16)

    def proj(idx):
        return _dot(xb, win_ref[:, idx * D_MODEL:(idx + 1) * D_MODEL])

    u = jax.nn.gelu(proj(0))
    vn = _layer_norm(jax.nn.gelu(proj(1)), lng_ref[...], lnb_ref[...]).astype(_BF16)
    tri = (lax.broadcasted_iota(jnp.int32, (GM_CHUNK, GM_CHUNK), 1)
           <= lax.broadcasted_iota(jnp.int32, (GM_CHUNK, GM_CHUNK), 0))
    ws = [jnp.where(tri, ws_ref[g], 0.0).astype(_BF16) for g in range(GM_GROUPS)]
    rows = []
    for c in range(PROJ_ROWS // GM_CHUNK):
        r0 = c * GM_CHUNK
        groups = [_dot(ws[g], vn[r0:r0 + GM_CHUNK, g * GM_GROUP_DIM:(g + 1) * GM_GROUP_DIM])
                  for g in range(GM_GROUPS)]
        rows.append(jnp.concatenate(groups, axis=1) + bs_ref[...])
    sv = jnp.concatenate(rows, axis=0)
    y_gm = _dot((u * sv).astype(_BF16), wgo_ref[...])
    yg_ref[...] = jax.nn.sigmoid(proj(5)) * y_gm
    sga_ref[...] = jax.nn.sigmoid(proj(6))

    cos_t, sin_lo, sin_hi = cos_ref[...], slo_ref[...], shi_ref[...]
    q = _rotary(proj(2), cos_t, sin_lo, sin_hi)
    q_ref[...] = (q * ATT_SCALE).astype(_BF16)
    k = _rotary(proj(3), cos_t, sin_lo, sin_hi)
    k_ref[...] = k.astype(_BF16)
    kmean_ref[0] = jnp.mean(k, axis=0, keepdims=True)
    vt_ref[...] = proj(4).T.astype(_BF16)


def _rope_tables(S):
    inv_freq = ROPE_THETA ** (-jnp.arange(0, ROT_DIM, 2, dtype=_F32) / ROT_DIM)
    ang = jnp.arange(S, dtype=jnp.int32).astype(_F32)[:, None] * inv_freq[None, :]
    cos, sin = jnp.cos(ang), jnp.sin(ang)
    half = ROT_DIM // 2
    ones = jnp.ones((S, HEAD_DIM - ROT_DIM), _F32)
    zeros_h = jnp.zeros((S, half), _F32)
    zeros_r = jnp.zeros((S, HEAD_DIM - ROT_DIM), _F32)
    cos_h = jnp.concatenate([cos, cos, ones], axis=1)
    lo_h = jnp.concatenate([-sin, zeros_h, zeros_r], axis=1)
    hi_h = jnp.concatenate([zeros_h, sin, zeros_r], axis=1)
    tile = lambda t: jnp.tile(t, (1, LANES // HEAD_DIM))
    return tile(cos_h), tile(lo_h), tile(hi_h)


def _proj_call(x1, win, lng, lnb, ws, bs_full, wgo):
    S = x1.shape[0]
    nb = S // MOBA_BLOCK
    cos_t, sin_lo, sin_hi = _rope_tables(S)
    row = pl.BlockSpec((PROJ_ROWS, D_MODEL), lambda i: (i, 0))
    tab = pl.BlockSpec((PROJ_ROWS, LANES), lambda i: (i, 0))
    out_shape = (
        jax.ShapeDtypeStruct((S, D_MODEL), _F32),
        jax.ShapeDtypeStruct((S, D_MODEL), _BF16),
        jax.ShapeDtypeStruct((S, D_MODEL), _BF16),
        jax.ShapeDtypeStruct((D_MODEL, S), _BF16),
        jax.ShapeDtypeStruct((nb, 1, D_MODEL), _F32),
        jax.ShapeDtypeStruct((S, D_MODEL), _F32),
    )
    out_specs = (
        row, row, row,
        pl.BlockSpec((D_MODEL, PROJ_ROWS), lambda i: (0, i)),
        pl.BlockSpec((1, 1, D_MODEL), lambda i: (i, 0, 0)),
        row,
    )
    return pl.pallas_call(
        _proj_kernel,
        grid=(S // PROJ_ROWS,),
        in_specs=[row, _resident(win.shape), _resident(lng.shape), _resident(lnb.shape),
                  _resident(ws.shape), _resident(bs_full.shape), _resident(wgo.shape),
                  tab, tab, tab],
        out_specs=out_specs,
        out_shape=out_shape,
        compiler_params=pltpu.CompilerParams(
            dimension_semantics=("arbitrary",), vmem_limit_bytes=VMEM_LIMIT),
        name="inproj_gmlp",
    )(x1, win, lng, lnb, ws, bs_full, wgo, cos_t, sin_lo, sin_hi)


def _nt_dot(a, b):
    return lax.dot_general(a, b, (((1,), (1,)), ((), ())), preferred_element_type=_F32)


def _moba_kernel(q_ref, k_ref, vt_ref, kmean_ref, o_ref, bias_ref):
    own = pl.program_id(1)
    nb = kmean_ref.shape[0]
    blk = MOBA_BLOCK
    own_start = pl.multiple_of(own * blk, blk)

    blk_iota = lax.broadcasted_iota(jnp.int32, (nb, blk), 0)
    key_iota = lax.broadcasted_iota(jnp.int32, (blk, blk), 0)
    qry_iota = lax.broadcasted_iota(jnp.int32, (blk, blk), 1)
    causal = key_iota <= qry_iota

    q_heads, state = [], []
    for h in range(HEADS_PER_STEP):
        cols = slice(h * HEAD_DIM, (h + 1) * HEAD_DIM)
        qh = q_ref[:, cols]
        q_heads.append(qh)

        gate = lax.dot_general(kmean_ref[:, cols], qh.astype(_F32), (((1,), (1,)), ((), ())),
                               preferred_element_type=_F32, precision=lax.Precision.HIGHEST)
        candidate = blk_iota < own
        gate = jnp.where(candidate, gate, -jnp.inf)
        chosen = jnp.zeros((nb, blk), dtype=jnp.bool_)
        for _ in range(MOBA_TOPK):
            best = jnp.max(gate, axis=0, keepdims=True)
            first = jnp.min(jnp.where(gate == best, blk_iota, nb), axis=0, keepdims=True)
            pick = blk_iota == first
            chosen = jnp.logical_or(chosen, pick)
            gate = jnp.where(pick, -jnp.inf, gate)
        chosen = jnp.logical_and(chosen, candidate)
        bias_ref[h] = jnp.where(chosen, 0.0, NEG_BIG)

        s = _nt_dot(k_ref[pl.ds(own_start, blk), cols], qh)
        s = jnp.where(causal, s, -jnp.inf)
        m = jnp.max(s, axis=0, keepdims=True)
        p = jnp.exp(s - m)
        l = jnp.sum(p, axis=0, keepdims=True)
        acc = _dot(vt_ref[h * HEAD_DIM:(h + 1) * HEAD_DIM, pl.ds(own_start, blk)], p.astype(_BF16))
        state.extend([m, l, acc])

    def body(j, carry):
        start = pl.multiple_of(j * blk, blk)
        new = []
        for h in range(HEADS_PER_STEP):
            cols = slice(h * HEAD_DIM, (h + 1) * HEAD_DIM)
            m, l, acc = carry[3 * h:3 * h + 3]
            s = _nt_dot(k_ref[pl.ds(start, blk), cols], q_heads[h]) + bias_ref[h, pl.ds(j, 1), :]
            m_new = jnp.maximum(m, jnp.max(s, axis=0, keepdims=True))
            alpha = jnp.exp(m - m_new)
            p = jnp.exp(s - m_new)
            l = alpha * l + jnp.sum(p, axis=0, keepdims=True)
            acc = alpha * acc + _dot(vt_ref[h * HEAD_DIM:(h + 1) * HEAD_DIM, pl.ds(start, blk)],
                                     p.astype(_BF16))
            new.extend([m_new, l, acc])
        return tuple(new)

    state = lax.fori_loop(0, own, body, tuple(state))
    out_t = jnp.concatenate([state[3 * h + 2] / state[3 * h + 1] for h in range(HEADS_PER_STEP)],
                            axis=0)
    o_ref[...] = out_t.T.astype(o_ref.dtype)


def _moba_call(q, k, vt, kmean):
    S = q.shape[0]
    nb = S // MOBA_BLOCK
    n_pairs = D_MODEL // LANES
    return pl.pallas_call(
        _moba_kernel,
        grid=(n_pairs, nb),
        in_specs=[
            pl.BlockSpec((MOBA_BLOCK, LANES), lambda hp, i: (i, hp)),
            pl.BlockSpec((S, LANES), lambda hp, i: (0, hp)),
            pl.BlockSpec((LANES, S), lambda hp, i: (hp, 0)),
            pl.BlockSpec((nb, LANES), lambda hp, i: (0, hp)),
        ],
        out_specs=pl.BlockSpec((MOBA_BLOCK, LANES), lambda hp, i: (i, hp)),
        out_shape=jax.ShapeDtypeStruct((S, D_MODEL), _BF16),
        scratch_shapes=[pltpu.VMEM((HEADS_PER_STEP, nb, MOBA_BLOCK), _F32)],
        compiler_params=pltpu.CompilerParams(
            dimension_semantics=("arbitrary", "arbitrary"), vmem_limit_bytes=VMEM_LIMIT),
        name="moba_attention",
    )(q, k, vt, kmean)


def _out_kernel(x1_ref, yatt_ref, yg_ref, sga_ref, wao_ref, wo_ref, g2_ref, b2_ref,
                wg_ref, wu_ref, wd_ref, g3_ref, b3_ref, o_ref):
    y_att = _dot(yatt_ref[...], wao_ref[...])
    merged = (yg_ref[...] + sga_ref[...] * y_att).astype(_BF16)
    z = DEEPNORM_ALPHA * x1_ref[...] + _dot(merged, wo_ref[...])
    x2 = _layer_norm(z, g2_ref[...], b2_ref[...])
    o_ref[...] = _ffn_ln(x2, wg_ref, wu_ref, wd_ref, g3_ref[...], b3_ref[...])


def _out_call(x1, yatt, yg, sga, wao, wo, g2, b2, wg, wu, wd, g3, b3):
    S = x1.shape[0]
    row = pl.BlockSpec((FFN_ROWS, D_MODEL), lambda i: (i, 0))
    weights = (wao, wo, g2, b2, wg, wu, wd, g3, b3)
    return pl.pallas_call(
        _out_kernel,
        grid=(S // FFN_ROWS,),
        in_specs=[row, row, row, row] + [_resident(w.shape) for w in weights],
        out_specs=row,
        out_shape=jax.ShapeDtypeStruct((S, D_MODEL), _F32),
        compiler_params=pltpu.CompilerParams(
            dimension_semantics=("arbitrary",), vmem_limit_bytes=VMEM_LIMIT),
        name="merge_ffn2",
    )(x1, yatt, yg, sga, *weights)


def kernel(x, ffn1_w_gate, ffn1_w_up, ffn1_w_down, ln1_g, ln1_b, w_in, gm_ln_g, gm_ln_b, gm_w_s, gm_b_s, w_gm_out, w_att_out, w_o, ln2_g, ln2_b, ffn2_w_gate, ffn2_w_up, ffn2_w_down, ln3_g, ln3_b):
    B, S, D = x.shape
    assert D == D_MODEL and S % FFN_ROWS == 0 and S % MOBA_BLOCK == 0
    bf = lambda w: w.astype(_BF16)
    outs = []
    for b in range(B):
        xb = x[b]
        for l in range(DEPTH):
            x1 = _ffn1_call(xb, bf(ffn1_w_gate[l]), bf(ffn1_w_up[l]), bf(ffn1_w_down[l]),
                            ln1_g[l][None], ln1_b[l][None])
            bs_full = jnp.repeat(gm_b_s[l].T, GM_GROUP_DIM, axis=1)
            yg, q, k, vt, kmean, sga = _proj_call(
                x1, bf(w_in[l]), gm_ln_g[l][None], gm_ln_b[l][None], gm_w_s[l], bs_full,
                bf(w_gm_out[l]))
            yatt = _moba_call(q, k, vt, kmean.reshape(S // MOBA_BLOCK, D_MODEL))
            xb = _out_call(x1, yatt, yg, sga, bf(w_att_out[l]), bf(w_o[l]),
                           ln2_g[l][None], ln2_b[l][None],
                           bf(ffn2_w_gate[l]), bf(ffn2_w_up[l]), bf(ffn2_w_down[l]),
                           ln3_g[l][None], ln3_b[l][None])
        outs.append(xb)
    return jnp.stack(outs, axis=0)
```

```python
import functools

import jax
import jax.numpy as jnp
import numpy as np
from jax import lax
from jax.experimental import pallas as pl
from jax.experimental.pallas import tpu as pltpu

D_MODEL = 1024
DEPTH = 1
GM_GROUPS = 8
GM_CHUNK = 128
GM_GROUP_DIM = D_MODEL // GM_GROUPS
N_HEADS = 16
HEAD_DIM = 64
MOBA_BLOCK = 256
MOBA_TOPK = 3
ROPE_THETA = 500000.0
ROT_DIM = HEAD_DIM // 4
D_FF = 2816
LN_EPS = 1e-5
DEEPNORM_ALPHA = (2.0 * DEPTH) ** 0.25
ATT_SCALE = HEAD_DIM ** -0.5
LOG2_E = 1.4426950408889634
Q_SCALE = ATT_SCALE * LOG2_E

LANES = 128
FFN_ROWS = 512
FFN_CHUNKS = ((0, 1024), (1024, 1024), (2048, 768))
PROJ_ROWS = 256
HEADS_PER_STEP = LANES // HEAD_DIM
MOBA_GROUP = 2
BF16_SUBLANES = 16
VT_HEAD_ROWS = HEAD_DIM + BF16_SUBLANES
VMEM_LIMIT = 56 * 1024 * 1024
NEG_BIG = -1e30

_BF16 = jnp.bfloat16
_F32 = jnp.float32


def _resident(shape):
    return pl.BlockSpec(shape, lambda *_: (0,) * len(shape), pipeline_mode=pl.Buffered(1))


def _layer_norm(z, g, b):
    mu = jnp.mean(z, axis=-1, keepdims=True)
    zc = z - mu
    var = jnp.mean(zc * zc, axis=-1, keepdims=True)
    return zc * lax.rsqrt(var + LN_EPS) * g + b


def _dot(a, b):
    return jnp.dot(a, b, preferred_element_type=_F32)


def _ffn_ln(x, wg_ref, wu_ref, wd_ref, g, b):
    xb = x.astype(_BF16)
    y = None
    for start, width in FFN_CHUNKS:
        gate = _dot(xb, wg_ref[:, start:start + width])
        up = _dot(xb, wu_ref[:, start:start + width])
        h = (jax.nn.silu(gate) * up).astype(_BF16)
        part = _dot(h, wd_ref[start:start + width, :])
        y = part if y is None else y + part
    return _layer_norm(DEEPNORM_ALPHA * x + 0.5 * y, g, b)


def _ffn1_kernel(x_ref, wg_ref, wu_ref, wd_ref, g_ref, b_ref, o_ref):
    o_ref[...] = _ffn_ln(x_ref[...], wg_ref, wu_ref, wd_ref, g_ref[...], b_ref[...])


def _ffn1_call(x, wg, wu, wd, g, b):
    S = x.shape[0]
    row = pl.BlockSpec((FFN_ROWS, D_MODEL), lambda i: (i, 0))
    return pl.pallas_call(
        _ffn1_kernel,
        grid=(S // FFN_ROWS,),
        in_specs=[row, _resident(wg.shape), _resident(wu.shape), _resident(wd.shape),
                  _resident(g.shape), _resident(b.shape)],
        out_specs=row,
        out_shape=jax.ShapeDtypeStruct((S, D_MODEL), _F32),
        compiler_params=pltpu.CompilerParams(
            dimension_semantics=("arbitrary",), vmem_limit_bytes=VMEM_LIMIT),
        name="ffn1_ln1",
    )(x, wg, wu, wd, g, b)


def _rotary(t, cos_t, sin_lo, sin_hi):
    half = ROT_DIM // 2
    cols = []
    for j in range(D_MODEL // LANES):
        tj = t[:, j * LANES:(j + 1) * LANES]
        fwd = pltpu.roll(tj, LANES - half, axis=1)
        bwd = pltpu.roll(tj, half, axis=1)
        cols.append(tj * cos_t + fwd * sin_lo + bwd * sin_hi)
    return jnp.concatenate(cols, axis=1)


def _proj_kernel(x_ref, win_ref, lng_ref, lnb_ref, ws_ref, bs_ref, wgo_ref,
                 cos_ref, slo_ref, shi_ref,
                 yg_ref, q_ref, k_ref, vt_ref, kmean_ref, sga_ref):
    xb = x_ref[...].astype(_BF16)

    def proj(idx):
        return _dot(xb, win_ref[:, idx * D_MODEL:(idx + 1) * D_MODEL])

    u = jax.nn.gelu(proj(0))
    vn = _layer_norm(jax.nn.gelu(proj(1)), lng_ref[...], lnb_ref[...]).astype(_BF16)
    tri = (lax.broadcasted_iota(jnp.int32, (GM_CHUNK, GM_CHUNK), 1)
           <= lax.broadcasted_iota(jnp.int32, (GM_CHUNK, GM_CHUNK), 0))
    ws = [jnp.where(tri, ws_ref[g], 0.0).astype(_BF16) for g in range(GM_GROUPS)]
    rows = []
    for c in range(PROJ_ROWS // GM_CHUNK):
        r0 = c * GM_CHUNK
        groups = [_dot(ws[g], vn[r0:r0 + GM_CHUNK, g * GM_GROUP_DIM:(g + 1) * GM_GROUP_DIM])
                  for g in range(GM_GROUPS)]
        rows.append(jnp.concatenate(groups, axis=1) + bs_ref[...])
    sv = jnp.concatenate(rows, axis=0)
    y_gm = _dot((u * sv).astype(_BF16), wgo_ref[...])
    yg_ref[...] = jax.nn.sigmoid(proj(5)) * y_gm
    sga_ref[...] = jax.nn.sigmoid(proj(6))

    cos_t, sin_lo, sin_hi = cos_ref[...], slo_ref[...], shi_ref[...]
    q = _rotary(proj(2), cos_t, sin_lo, sin_hi)
    q_ref[...] = (q * Q_SCALE).astype(_BF16)
    k = _rotary(proj(3), cos_t, sin_lo, sin_hi)
    k_ref[...] = k.astype(_BF16)
    kmean_ref[0] = jnp.mean(k, axis=0, keepdims=True)
    v_t = proj(4).T.astype(_BF16)
    ones = jnp.ones((VT_HEAD_ROWS - HEAD_DIM, PROJ_ROWS), _BF16)
    for h in range(N_HEADS):
        vt_ref[h * VT_HEAD_ROWS:h * VT_HEAD_ROWS + HEAD_DIM, :] = v_t[h * HEAD_DIM:(h + 1) * HEAD_DIM, :]
        vt_ref[h * VT_HEAD_ROWS + HEAD_DIM:(h + 1) * VT_HEAD_ROWS, :] = ones


def _rope_tables(S):
    inv_freq = ROPE_THETA ** (-jnp.arange(0, ROT_DIM, 2, dtype=_F32) / ROT_DIM)
    ang = jnp.arange(S, dtype=jnp.int32).astype(_F32)[:, None] * inv_freq[None, :]
    cos, sin = jnp.cos(ang), jnp.sin(ang)
    half = ROT_DIM // 2
    ones = jnp.ones((S, HEAD_DIM - ROT_DIM), _F32)
    zeros_h = jnp.zeros((S, half), _F32)
    zeros_r = jnp.zeros((S, HEAD_DIM - ROT_DIM), _F32)
    cos_h = jnp.concatenate([cos, cos, ones], axis=1)
    lo_h = jnp.concatenate([-sin, zeros_h, zeros_r], axis=1)
    hi_h = jnp.concatenate([zeros_h, sin, zeros_r], axis=1)
    tile = lambda t: jnp.tile(t, (1, LANES // HEAD_DIM))
    return tile(cos_h), tile(lo_h), tile(hi_h)


def _proj_call(x1, win, lng, lnb, ws, bs_full, wgo):
    S = x1.shape[0]
    nb = S // MOBA_BLOCK
    cos_t, sin_lo, sin_hi = _rope_tables(S)
    row = pl.BlockSpec((PROJ_ROWS, D_MODEL), lambda i: (i, 0))
    tab = pl.BlockSpec((PROJ_ROWS, LANES), lambda i: (i, 0))
    out_shape = (
        jax.ShapeDtypeStruct((S, D_MODEL), _F32),
        jax.ShapeDtypeStruct((S, D_MODEL), _BF16),
        jax.ShapeDtypeStruct((S, D_MODEL), _BF16),
        jax.ShapeDtypeStruct((N_HEADS * VT_HEAD_ROWS, S), _BF16),
        jax.ShapeDtypeStruct((nb, 1, D_MODEL), _F32),
        jax.ShapeDtypeStruct((S, D_MODEL), _F32),
    )
    out_specs = (
        row, row, row,
        pl.BlockSpec((N_HEADS * VT_HEAD_ROWS, PROJ_ROWS), lambda i: (0, i)),
        pl.BlockSpec((1, 1, D_MODEL), lambda i: (i, 0, 0)),
        row,
    )
    return pl.pallas_call(
        _proj_kernel,
        grid=(S // PROJ_ROWS,),
        in_specs=[row, _resident(win.shape), _resident(lng.shape), _resident(lnb.shape),
                  _resident(ws.shape), _resident(bs_full.shape), _resident(wgo.shape),
                  tab, tab, tab],
        out_specs=out_specs,
        out_shape=out_shape,
        compiler_params=pltpu.CompilerParams(
            dimension_semantics=("arbitrary",), vmem_limit_bytes=VMEM_LIMIT),
        name="inproj_gmlp",
    )(x1, win, lng, lnb, ws, bs_full, wgo, cos_t, sin_lo, sin_hi)


def _nt_dot(a, b):
    return lax.dot_general(a, b, (((1,), (1,)), ((), ())), preferred_element_type=_F32)


def _col_max(s):
    return jnp.max(s, axis=0, keepdims=True)


def _moba_kernel(q_ref, k_ref, hot_ref, vt_ref, kmean_ref, o_ref, sa_ref, sb_ref):
    own = pl.program_id(1)
    nb = kmean_ref.shape[0]
    blk, group = MOBA_BLOCK, MOBA_GROUP
    span = group * blk
    own_start = pl.multiple_of(own * blk, blk)
    n_grp = (own + group - 1) // group

    blk_iota = lax.broadcasted_iota(jnp.int32, (nb, blk), 0)
    causal = (lax.broadcasted_iota(jnp.int32, (blk, blk), 0)
              <= lax.broadcasted_iota(jnp.int32, (blk, blk), 1))
    lane_head = lax.broadcasted_iota(jnp.int32, (blk, LANES), 1) // HEAD_DIM

    def keys_with_block_id(start, rows):
        return jnp.concatenate([k_ref[pl.ds(start, rows), :], hot_ref[pl.ds(start, rows), :]], axis=1)

    def v_rows(h, start, rows):
        return vt_ref[h * VT_HEAD_ROWS:(h + 1) * VT_HEAD_ROWS, pl.ds(start, rows)]

    q_pair = q_ref[...]
    k_own = k_ref[pl.ds(own_start, blk), :]
    kmean = kmean_ref[...]
    km_hi = kmean.astype(_BF16)
    km_rest = kmean - km_hi.astype(_F32)
    km_mid = km_rest.astype(_BF16)
    km_lo = (km_rest - km_mid.astype(_F32)).astype(_BF16)
    km_terms = jnp.concatenate([km_hi, km_mid, km_lo], axis=1)
    q_aug, state = [], []
    for h in range(HEADS_PER_STEP):
        qz = jnp.where(lane_head == h, q_pair, jnp.zeros_like(q_pair))

        gate = _nt_dot(km_terms, jnp.concatenate([qz, qz, qz], axis=1))
        candidate = blk_iota < own
        gate = jnp.where(candidate, gate, -jnp.inf)
        chosen = jnp.zeros((nb, blk), dtype=jnp.bool_)
        for _ in range(MOBA_TOPK):
            best = _col_max(gate)
            first = jnp.min(jnp.where(gate == best, blk_iota, nb), axis=0, keepdims=True)
            pick = blk_iota == first
            chosen = jnp.logical_or(chosen, pick)
            gate = jnp.where(pick, -jnp.inf, gate)
        chosen = jnp.logical_and(chosen, candidate)
        bias = jnp.concatenate([jnp.where(chosen, 0.0, NEG_BIG),
                                jnp.zeros((LANES - nb, blk), _F32)], axis=0)
        q_aug.append(jnp.concatenate([qz, bias.T.astype(_BF16)], axis=1))

        s = jnp.where(causal, _nt_dot(k_own, qz), -jnp.inf)
        m = _col_max(s)
        acc = _dot(v_rows(h, own_start, blk), jnp.exp2(s - m).astype(_BF16))
        state.append((m, acc))

    last_group = nb // group - 1

    def score_group(g, dst_ref):
        start = pl.multiple_of(jnp.minimum(g, last_group) * span, span)
        k_grp = keys_with_block_id(start, span)
        scores = [_nt_dot(k_grp, q_aug[h]) for h in range(HEADS_PER_STEP)]
        maxes = []
        for h in range(HEADS_PER_STEP):
            dst_ref[h] = scores[h]
            maxes.append(_col_max(scores[h]))
        return maxes

    def softmax_step(g, src_ref, maxes, running):
        start = pl.multiple_of(g * span, span)
        new = []
        for h in range(HEADS_PER_STEP):
            m, acc = running[2 * h:2 * h + 2]
            m_new = jnp.maximum(m, maxes[h])
            alpha = jnp.exp2(m - m_new)
            p = jnp.exp2((src_ref[h] - m_new).astype(_BF16))
            new.extend([m_new, alpha * acc + _dot(v_rows(h, start, span), p)])
        return new

    max_a = score_group(0, sa_ref)

    def body(it, carry):
        running, max_a = list(carry[:2 * HEADS_PER_STEP]), list(carry[2 * HEADS_PER_STEP:])
        g = 2 * it
        max_b = score_group(g + 1, sb_ref)
        running = softmax_step(g, sa_ref, max_a, running)
        max_a = score_group(g + 2, sa_ref)
        running = softmax_step(g + 1, sb_ref, max_b, running)
        return tuple(running + max_a)

    init = [x for pair in state for x in pair] + max_a
    carry = lax.fori_loop(0, (n_grp + 1) // 2, body, tuple(init))
    out_t = jnp.concatenate(
        [carry[2 * h + 1][:HEAD_DIM] / carry[2 * h + 1][HEAD_DIM:HEAD_DIM + 1]
         for h in range(HEADS_PER_STEP)], axis=0)
    o_ref[...] = out_t.T.astype(o_ref.dtype)


def _moba_call(q, k, vt, kmean):
    S = q.shape[0]
    nb = S // MOBA_BLOCK
    assert nb <= LANES
    n_pairs = D_MODEL // LANES
    hot = (jnp.arange(S, dtype=jnp.int32)[:, None] // MOBA_BLOCK
           == jnp.arange(LANES, dtype=jnp.int32)[None, :]).astype(_BF16)
    pair_rows = HEADS_PER_STEP * VT_HEAD_ROWS
    return pl.pallas_call(
        _moba_kernel,
        grid=(n_pairs, nb),
        in_specs=[
            pl.BlockSpec((MOBA_BLOCK, LANES), lambda hp, i: (i, hp)),
            pl.BlockSpec((S, LANES), lambda hp, i: (0, hp)),
            pl.BlockSpec((S, LANES), lambda hp, i: (0, 0)),
            pl.BlockSpec((pair_rows, S), lambda hp, i: (hp, 0)),
            pl.BlockSpec((nb, LANES), lambda hp, i: (0, hp)),
        ],
        out_specs=pl.BlockSpec((MOBA_BLOCK, LANES), lambda hp, i: (i, hp)),
        out_shape=jax.ShapeDtypeStruct((S, D_MODEL), _BF16),
        scratch_shapes=[pltpu.VMEM((HEADS_PER_STEP, MOBA_GROUP * MOBA_BLOCK, MOBA_BLOCK), _F32)] * 2,
        compiler_params=pltpu.CompilerParams(
            dimension_semantics=("arbitrary", "arbitrary"), vmem_limit_bytes=VMEM_LIMIT),
        name="moba_attention",
    )(q, k, hot, vt, kmean)


def _out_kernel(x1_ref, yatt_ref, yg_ref, sga_ref, wao_ref, wo_ref, g2_ref, b2_ref,
                wg_ref, wu_ref, wd_ref, g3_ref, b3_ref, o_ref):
    y_att = _dot(yatt_ref[...], wao_ref[...])
    merged = (yg_ref[...] + sga_ref[...] * y_att).astype(_BF16)
    z = DEEPNORM_ALPHA * x1_ref[...] + _dot(merged, wo_ref[...])
    x2 = _layer_norm(z, g2_ref[...], b2_ref[...])
    o_ref[...] = _ffn_ln(x2, wg_ref, wu_ref, wd_ref, g3_ref[...], b3_ref[...])


def _out_call(x1, yatt, yg, sga, wao, wo, g2, b2, wg, wu, wd, g3, b3):
    S = x1.shape[0]
    row = pl.BlockSpec((FFN_ROWS, D_MODEL), lambda i: (i, 0))
    weights = (wao, wo, g2, b2, wg, wu, wd, g3, b3)
    return pl.pallas_call(
        _out_kernel,
        grid=(S // FFN_ROWS,),
        in_specs=[row, row, row, row] + [_resident(w.shape) for w in weights],
        out_specs=row,
        out_shape=jax.ShapeDtypeStruct((S, D_MODEL), _F32),
        compiler_params=pltpu.CompilerParams(
            dimension_semantics=("arbitrary",), vmem_limit_bytes=VMEM_LIMIT),
        name="merge_ffn2",
    )(x1, yatt, yg, sga, *weights)


def kernel(x, ffn1_w_gate, ffn1_w_up, ffn1_w_down, ln1_g, ln1_b, w_in, gm_ln_g, gm_ln_b, gm_w_s, gm_b_s, w_gm_out, w_att_out, w_o, ln2_g, ln2_b, ffn2_w_gate, ffn2_w_up, ffn2_w_down, ln3_g, ln3_b):
    B, S, D = x.shape
    assert D == D_MODEL and S % FFN_ROWS == 0 and S % (MOBA_BLOCK * MOBA_GROUP) == 0
    bf = lambda w: w.astype(_BF16)
    outs = []
    for b in range(B):
        xb = x[b]
        for l in range(DEPTH):
            x1 = _ffn1_call(xb, bf(ffn1_w_gate[l]), bf(ffn1_w_up[l]), bf(ffn1_w_down[l]),
                            ln1_g[l][None], ln1_b[l][None])
            bs_full = jnp.repeat(gm_b_s[l].T, GM_GROUP_DIM, axis=1)
            yg, q, k, vt, kmean, sga = _proj_call(
                x1, bf(w_in[l]), gm_ln_g[l][None], gm_ln_b[l][None], gm_w_s[l], bs_full,
                bf(w_gm_out[l]))
            yatt = _moba_call(q, k, vt, kmean.reshape(S // MOBA_BLOCK, D_MODEL))
            xb = _out_call(x1, yatt, yg, sga, bf(w_att_out[l]), bf(w_o[l]),
                           ln2_g[l][None], ln2_b[l][None],
                           bf(ffn2_w_gate[l]), bf(ffn2_w_up[l]), bf(ffn2_w_down[l]),
                           ln3_g[l][None], ln3_b[l][None])
        outs.append(xb)
    return jnp.stack(outs, axis=0)
```

```python
import functools

import jax
import jax.numpy as jnp
import numpy as np
from jax import lax
from jax.experimental import pallas as pl
from jax.experimental.pallas import tpu as pltpu

D_MODEL = 1024
DEPTH = 1
GM_GROUPS = 8
GM_CHUNK = 128
GM_GROUP_DIM = D_MODEL // GM_GROUPS
N_HEADS = 16
HEAD_DIM = 64
MOBA_BLOCK = 256
MOBA_TOPK = 3
ROPE_THETA = 500000.0
ROT_DIM = HEAD_DIM // 4
D_FF = 2816
LN_EPS = 1e-5
DEEPNORM_ALPHA = (2.0 * DEPTH) ** 0.25
ATT_SCALE = HEAD_DIM ** -0.5
LOG2_E = 1.4426950408889634
Q_SCALE = ATT_SCALE * LOG2_E

LANES = 128
FFN_ROWS = 512
FFN_CHUNKS = ((0, 1024), (1024, 1024), (2048, 768))
PROJ_ROWS = 256
HEADS_PER_PAIR = LANES // HEAD_DIM
PAIRS_PER_STEP = 2
HEADS_PER_STEP = HEADS_PER_PAIR * PAIRS_PER_STEP
STEP_LANES = PAIRS_PER_STEP * LANES
MOBA_GROUP = 4
BF16_SUBLANES = 16
VT_HEAD_ROWS = HEAD_DIM + BF16_SUBLANES
VMEM_LIMIT = 56 * 1024 * 1024
NEG_BIG = -1e30

_BF16 = jnp.bfloat16
_F32 = jnp.float32


def _resident(shape):
    return pl.BlockSpec(shape, lambda *_: (0,) * len(shape), pipeline_mode=pl.Buffered(1))


def _layer_norm(z, g, b):
    mu = jnp.mean(z, axis=-1, keepdims=True)
    zc = z - mu
    var = jnp.mean(zc * zc, axis=-1, keepdims=True)
    return zc * lax.rsqrt(var + LN_EPS) * g + b


def _dot(a, b):
    return jnp.dot(a, b, preferred_element_type=_F32)


def _ffn_ln(x, wg_ref, wu_ref, wd_ref, g, b):
    xb = x.astype(_BF16)
    y = None
    for start, width in FFN_CHUNKS:
        gate = _dot(xb, wg_ref[:, start:start + width])
        up = _dot(xb, wu_ref[:, start:start + width])
        h = (jax.nn.silu(gate) * up).astype(_BF16)
        part = _dot(h, wd_ref[start:start + width, :])
        y = part if y is None else y + part
    return _layer_norm(DEEPNORM_ALPHA * x + 0.5 * y, g, b)


def _ffn1_kernel(x_ref, wg_ref, wu_ref, wd_ref, g_ref, b_ref, o_ref):
    o_ref[...] = _ffn_ln(x_ref[...], wg_ref, wu_ref, wd_ref, g_ref[...], b_ref[...])


def _ffn1_call(x, wg, wu, wd, g, b):
    S = x.shape[0]
    row = pl.BlockSpec((FFN_ROWS, D_MODEL), lambda i: (i, 0))
    return pl.pallas_call(
        _ffn1_kernel,
        grid=(S // FFN_ROWS,),
        in_specs=[row, _resident(wg.shape), _resident(wu.shape), _resident(wd.shape),
                  _resident(g.shape), _resident(b.shape)],
        out_specs=row,
        out_shape=jax.ShapeDtypeStruct((S, D_MODEL), _F32),
        compiler_params=pltpu.CompilerParams(
            dimension_semantics=("arbitrary",), vmem_limit_bytes=VMEM_LIMIT),
        name="ffn1_ln1",
    )(x, wg, wu, wd, g, b)


def _rotary(t, cos_t, sin_lo, sin_hi):
    half = ROT_DIM // 2
    cols = []
    for j in range(D_MODEL // LANES):
        tj = t[:, j * LANES:(j + 1) * LANES]
        fwd = pltpu.roll(tj, LANES - half, axis=1)
        bwd = pltpu.roll(tj, half, axis=1)
        cols.append(tj * cos_t + fwd * sin_lo + bwd * sin_hi)
    return jnp.concatenate(cols, axis=1)


def _proj_kernel(x_ref, win_ref, lng_ref, lnb_ref, ws_ref, bs_ref, wgo_ref,
                 cos_ref, slo_ref, shi_ref,
                 yg_ref, q_ref, k_ref, vt_ref, kmean_ref, sga_ref):
    xb = x_ref[...].astype(_BF16)

    def proj(idx):
        return _dot(xb, win_ref[:, idx * D_MODEL:(idx + 1) * D_MODEL])

    u = jax.nn.gelu(proj(0))
    vn = _layer_norm(jax.nn.gelu(proj(1)), lng_ref[...], lnb_ref[...]).astype(_BF16)
    tri = (lax.broadcasted_iota(jnp.int32, (GM_CHUNK, GM_CHUNK), 1)
           <= lax.broadcasted_iota(jnp.int32, (GM_CHUNK, GM_CHUNK), 0))
    ws = [jnp.where(tri, ws_ref[g], 0.0).astype(_BF16) for g in range(GM_GROUPS)]
    rows = []
    for c in range(PROJ_ROWS // GM_CHUNK):
        r0 = c * GM_CHUNK
        groups = [_dot(ws[g], vn[r0:r0 + GM_CHUNK, g * GM_GROUP_DIM:(g + 1) * GM_GROUP_DIM])
                  for g in range(GM_GROUPS)]
        rows.append(jnp.concatenate(groups, axis=1) + bs_ref[...])
    sv = jnp.concatenate(rows, axis=0)
    y_gm = _dot((u * sv).astype(_BF16), wgo_ref[...])
    yg_ref[...] = jax.nn.sigmoid(proj(5)) * y_gm
    sga_ref[...] = jax.nn.sigmoid(proj(6))

    cos_t, sin_lo, sin_hi = cos_ref[...], slo_ref[...], shi_ref[...]
    q = _rotary(proj(2), cos_t, sin_lo, sin_hi)
    q_ref[...] = (q * Q_SCALE).astype(_BF16)
    k = _rotary(proj(3), cos_t, sin_lo, sin_hi)
    k_ref[...] = k.astype(_BF16)
    kmean_ref[0] = jnp.mean(k, axis=0, keepdims=True)
    v_t = proj(4).T.astype(_BF16)
    ones = jnp.ones((VT_HEAD_ROWS - HEAD_DIM, PROJ_ROWS), _BF16)
    for h in range(N_HEADS):
        vt_ref[h * VT_HEAD_ROWS:h * VT_HEAD_ROWS + HEAD_DIM, :] = v_t[h * HEAD_DIM:(h + 1) * HEAD_DIM, :]
        vt_ref[h * VT_HEAD_ROWS + HEAD_DIM:(h + 1) * VT_HEAD_ROWS, :] = ones


def _rope_tables(S):
    inv_freq = ROPE_THETA ** (-jnp.arange(0, ROT_DIM, 2, dtype=_F32) / ROT_DIM)
    ang = jnp.arange(S, dtype=jnp.int32).astype(_F32)[:, None] * inv_freq[None, :]
    cos, sin = jnp.cos(ang), jnp.sin(ang)
    half = ROT_DIM // 2
    ones = jnp.ones((S, HEAD_DIM - ROT_DIM), _F32)
    zeros_h = jnp.zeros((S, half), _F32)
    zeros_r = jnp.zeros((S, HEAD_DIM - ROT_DIM), _F32)
    cos_h = jnp.concatenate([cos, cos, ones], axis=1)
    lo_h = jnp.concatenate([-sin, zeros_h, zeros_r], axis=1)
    hi_h = jnp.concatenate([zeros_h, sin, zeros_r], axis=1)
    tile = lambda t: jnp.tile(t, (1, LANES // HEAD_DIM))
    return tile(cos_h), tile(lo_h), tile(hi_h)


def _proj_call(x1, win, lng, lnb, ws, bs_full, wgo):
    S = x1.shape[0]
    nb = S // MOBA_BLOCK
    cos_t, sin_lo, sin_hi = _rope_tables(S)
    row = pl.BlockSpec((PROJ_ROWS, D_MODEL), lambda i: (i, 0))
    tab = pl.BlockSpec((PROJ_ROWS, LANES), lambda i: (i, 0))
    out_shape = (
        jax.ShapeDtypeStruct((S, D_MODEL), _F32),
        jax.ShapeDtypeStruct((S, D_MODEL), _BF16),
        jax.ShapeDtypeStruct((S, D_MODEL), _BF16),
        jax.ShapeDtypeStruct((N_HEADS * VT_HEAD_ROWS, S), _BF16),
        jax.ShapeDtypeStruct((nb, 1, D_MODEL), _F32),
        jax.ShapeDtypeStruct((S, D_MODEL), _F32),
    )
    out_specs = (
        row, row, row,
        pl.BlockSpec((N_HEADS * VT_HEAD_ROWS, PROJ_ROWS), lambda i: (0, i)),
        pl.BlockSpec((1, 1, D_MODEL), lambda i: (i, 0, 0)),
        row,
    )
    return pl.pallas_call(
        _proj_kernel,
        grid=(S // PROJ_ROWS,),
        in_specs=[row, _resident(win.shape), _resident(lng.shape), _resident(lnb.shape),
                  _resident(ws.shape), _resident(bs_full.shape), _resident(wgo.shape),
                  tab, tab, tab],
        out_specs=out_specs,
        out_shape=out_shape,
        compiler_params=pltpu.CompilerParams(
            dimension_semantics=("arbitrary",), vmem_limit_bytes=VMEM_LIMIT),
        name="inproj_gmlp",
    )(x1, win, lng, lnb, ws, bs_full, wgo, cos_t, sin_lo, sin_hi)


def _nt_dot(a, b):
    return lax.dot_general(a, b, (((1,), (1,)), ((), ())), preferred_element_type=_F32)


def _col_max(s):
    return jnp.max(s, axis=0, keepdims=True)


def _moba_kernel(q_ref, k_ref, hot_ref, causal_ref, vt_ref, kmean_ref, o_ref,
                 qa_ref, s_ref, smax_ref, m_ref, acc_ref):
    own = pl.program_id(1)
    nb = kmean_ref.shape[0]
    blk, group = MOBA_BLOCK, MOBA_GROUP
    span = group * blk
    n_full = own // group

    blk_iota = lax.broadcasted_iota(jnp.int32, (nb, blk), 0)
    lane_head = lax.broadcasted_iota(jnp.int32, (blk, LANES), 1) // HEAD_DIM

    def pair_lanes(pair):
        return slice(pair * LANES, (pair + 1) * LANES)

    def keys_with_block_id(pair, start, rows):
        return jnp.concatenate([k_ref[pl.ds(start, rows), pair_lanes(pair)],
                                hot_ref[pl.ds(start, rows), :]], axis=1)

    def v_rows(h, start, rows):
        return vt_ref[h * VT_HEAD_ROWS:(h + 1) * VT_HEAD_ROWS, pl.ds(start, rows)]

    kmean = kmean_ref[...]
    km_hi = kmean.astype(_BF16)
    km_rest = kmean - km_hi.astype(_F32)
    km_mid = km_rest.astype(_BF16)
    km_lo = (km_rest - km_mid.astype(_F32)).astype(_BF16)
    for h in range(HEADS_PER_STEP):
        pair = h // HEADS_PER_PAIR
        q_pair = q_ref[:, pair_lanes(pair)]
        km_terms = jnp.concatenate([t[:, pair_lanes(pair)] for t in (km_hi, km_mid, km_lo)],
                                   axis=1)
        qz = jnp.where(lane_head == h % HEADS_PER_PAIR, q_pair, jnp.zeros_like(q_pair))

        gate = _nt_dot(km_terms, jnp.concatenate([qz, qz, qz], axis=1))
        candidate = blk_iota < own
        gate = jnp.where(candidate, gate, -jnp.inf)
        chosen = jnp.zeros((nb, blk), dtype=jnp.bool_)
        for _ in range(MOBA_TOPK):
            best = _col_max(gate)
            first = jnp.min(jnp.where(gate == best, blk_iota, nb), axis=0, keepdims=True)
            pick = blk_iota == first
            chosen = jnp.logical_or(chosen, pick)
            gate = jnp.where(pick, -jnp.inf, gate)
        chosen = jnp.logical_or(jnp.logical_and(chosen, candidate), blk_iota == own)
        bias = jnp.concatenate([jnp.where(chosen, 0.0, NEG_BIG),
                                jnp.zeros((LANES - nb, blk), _F32)], axis=0)
        qa_ref[h] = jnp.concatenate([qz, bias.T.astype(_BF16)], axis=1)
        m_ref[h] = jnp.full((1, blk), -jnp.inf, _F32)
        acc_ref[h] = jnp.zeros((VT_HEAD_ROWS, blk), _F32)

    last = HEADS_PER_STEP - 1

    def score_head(g, h):
        start = pl.multiple_of(g * span, span)
        s = _nt_dot(keys_with_block_id(h // HEADS_PER_PAIR, start, span), qa_ref[h])
        s_ref[h] = s
        smax_ref[h] = _col_max(s)

    def attend_head(g, h, s, s_max):
        m = m_ref[h]
        m_new = jnp.maximum(m, s_max)
        p = jnp.exp2((s - m_new).astype(_BF16))
        v_grp = v_rows(h, pl.multiple_of(g * span, span), span)
        acc_ref[h] = jnp.exp2(m - m_new) * acc_ref[h] + _dot(v_grp, p)
        m_ref[h] = m_new

    for h in range(last):
        score_head(0, h)

    def body(g, carry):
        score_head(g, last)
        for h in range(HEADS_PER_STEP):
            attend_head(g, h, s_ref[h], smax_ref[h])
            if h < last:
                score_head(g + 1, h)
        return carry

    lax.fori_loop(0, n_full, body, 0)

    score_head(n_full, last)
    own_in_group = own - n_full * group
    for h in range(HEADS_PER_STEP):
        s = s_ref[h] + causal_ref[own_in_group]
        attend_head(n_full, h, s, _col_max(s))
    out_t = jnp.concatenate(
        [acc_ref[h, :HEAD_DIM, :] / acc_ref[h, HEAD_DIM:HEAD_DIM + 1, :]
         for h in range(HEADS_PER_STEP)], axis=0)
    o_ref[...] = out_t.T.astype(o_ref.dtype)


def _moba_call(q, k, vt, kmean):
    S = q.shape[0]
    nb = S // MOBA_BLOCK
    assert nb <= LANES
    n_steps = D_MODEL // STEP_LANES
    hot = (jnp.arange(S, dtype=jnp.int32)[:, None] // MOBA_BLOCK
           == jnp.arange(LANES, dtype=jnp.int32)[None, :]).astype(_BF16)
    span = MOBA_GROUP * MOBA_BLOCK
    key_rel = (jnp.arange(span, dtype=jnp.int32)[None, :, None]
               - MOBA_BLOCK * jnp.arange(MOBA_GROUP, dtype=jnp.int32)[:, None, None])
    causal = jnp.where(key_rel <= jnp.arange(MOBA_BLOCK, dtype=jnp.int32)[None, None, :],
                       0.0, NEG_BIG).astype(_F32)
    step_rows = HEADS_PER_STEP * VT_HEAD_ROWS
    once = pl.Buffered(1)
    return pl.pallas_call(
        _moba_kernel,
        grid=(n_steps, nb),
        in_specs=[
            pl.BlockSpec((MOBA_BLOCK, STEP_LANES), lambda hg, i: (i, hg)),
            pl.BlockSpec((S, STEP_LANES), lambda hg, i: (0, hg), pipeline_mode=once),
            pl.BlockSpec((S, LANES), lambda hg, i: (0, 0), pipeline_mode=once),
            pl.BlockSpec(causal.shape, lambda hg, i: (0, 0, 0), pipeline_mode=once),
            pl.BlockSpec((step_rows, S), lambda hg, i: (hg, 0), pipeline_mode=once),
            pl.BlockSpec((nb, STEP_LANES), lambda hg, i: (0, hg)),
        ],
        out_specs=pl.BlockSpec((MOBA_BLOCK, STEP_LANES), lambda hg, i: (i, hg)),
        out_shape=jax.ShapeDtypeStruct((S, D_MODEL), _BF16),
        scratch_shapes=[
            pltpu.VMEM((HEADS_PER_STEP, MOBA_BLOCK, 2 * LANES), _BF16),
            pltpu.VMEM((HEADS_PER_STEP, MOBA_GROUP * MOBA_BLOCK, MOBA_BLOCK), _F32),
            pltpu.VMEM((HEADS_PER_STEP, 1, MOBA_BLOCK), _F32),
            pltpu.VMEM((HEADS_PER_STEP, 1, MOBA_BLOCK), _F32),
            pltpu.VMEM((HEADS_PER_STEP, VT_HEAD_ROWS, MOBA_BLOCK), _F32),
        ],
        compiler_params=pltpu.CompilerParams(
            dimension_semantics=("arbitrary", "arbitrary"), vmem_limit_bytes=VMEM_LIMIT),
        name="moba_attention",
    )(q, k, hot, causal, vt, kmean)


def _out_kernel(x1_ref, yatt_ref, yg_ref, sga_ref, wao_ref, wo_ref, g2_ref, b2_ref,
                wg_ref, wu_ref, wd_ref, g3_ref, b3_ref, o_ref):
    y_att = _dot(yatt_ref[...], wao_ref[...])
    merged = (yg_ref[...] + sga_ref[...] * y_att).astype(_BF16)
    z = DEEPNORM_ALPHA * x1_ref[...] + _dot(merged, wo_ref[...])
    x2 = _layer_norm(z, g2_ref[...], b2_ref[...])
    o_ref[...] = _ffn_ln(x2, wg_ref, wu_ref, wd_ref, g3_ref[...], b3_ref[...])


def _out_call(x1, yatt, yg, sga, wao, wo, g2, b2, wg, wu, wd, g3, b3):
    S = x1.shape[0]
    row = pl.BlockSpec((FFN_ROWS, D_MODEL), lambda i: (i, 0))
    weights = (wao, wo, g2, b2, wg, wu, wd, g3, b3)
    return pl.pallas_call(
        _out_kernel,
        grid=(S // FFN_ROWS,),
        in_specs=[row, row, row, row] + [_resident(w.shape) for w in weights],
        out_specs=row,
        out_shape=jax.ShapeDtypeStruct((S, D_MODEL), _F32),
        compiler_params=pltpu.CompilerParams(
            dimension_semantics=("arbitrary",), vmem_limit_bytes=VMEM_LIMIT),
        name="merge_ffn2",
    )(x1, yatt, yg, sga, *weights)


def kernel(x, ffn1_w_gate, ffn1_w_up, ffn1_w_down, ln1_g, ln1_b, w_in, gm_ln_g, gm_ln_b, gm_w_s, gm_b_s, w_gm_out, w_att_out, w_o, ln2_g, ln2_b, ffn2_w_gate, ffn2_w_up, ffn2_w_down, ln3_g, ln3_b):
    B, S, D = x.shape
    assert D == D_MODEL and S % FFN_ROWS == 0 and S % (MOBA_BLOCK * MOBA_GROUP) == 0
    bf = lambda w: w.astype(_BF16)
    outs = []
    for b in range(B):
        xb = x[b]
        for l in range(DEPTH):
            x1 = _ffn1_call(xb, bf(ffn1_w_gate[l]), bf(ffn1_w_up[l]), bf(ffn1_w_down[l]),
                            ln1_g[l][None], ln1_b[l][None])
            bs_full = jnp.repeat(gm_b_s[l].T, GM_GROUP_DIM, axis=1)
            yg, q, k, vt, kmean, sga = _proj_call(
                x1, bf(w_in[l]), gm_ln_g[l][None], gm_ln_b[l][None], gm_w_s[l], bs_full,
                bf(w_gm_out[l]))
            yatt = _moba_call(q, k, vt, kmean.reshape(S // MOBA_BLOCK, D_MODEL))
            xb = _out_call(x1, yatt, yg, sga, bf(w_att_out[l]), bf(w_o[l]),
                           ln2_g[l][None], ln2_b[l][None],
                           bf(ffn2_w_gate[l]), bf(ffn2_w_up[l]), bf(ffn2_w_down[l]),
                           ln3_g[l][None], ln3_b[l][None])
        outs.append(xb)
    return jnp.stack(outs, axis=0)
```

```python
import functools

import jax
import jax.numpy as jnp
import numpy as np
from jax import lax
from jax.experimental import pallas as pl
from jax.experimental.pallas import tpu as pltpu

D_MODEL = 1024
DEPTH = 1
GM_GROUPS = 8
GM_CHUNK = 128
GM_GROUP_DIM = D_MODEL // GM_GROUPS
N_HEADS = 16
HEAD_DIM = 64
MOBA_BLOCK = 256
MOBA_TOPK = 3
ROPE_THETA = 500000.0
ROT_DIM = HEAD_DIM // 4
D_FF = 2816
LN_EPS = 1e-5
DEEPNORM_ALPHA = (2.0 * DEPTH) ** 0.25
ATT_SCALE = HEAD_DIM ** -0.5
LOG2_E = 1.4426950408889634
Q_SCALE = ATT_SCALE * LOG2_E

LANES = 128
FFN_ROWS = 512
FFN_CHUNKS = ((0, 1024), (1024, 1024), (2048, 768))
PROJ_ROWS = 256
HEADS_PER_PAIR = LANES // HEAD_DIM
PAIRS_PER_STEP = 4
HEADS_PER_STEP = HEADS_PER_PAIR * PAIRS_PER_STEP
STEP_LANES = PAIRS_PER_STEP * LANES
MOBA_GROUP = 4
BF16_SUBLANES = 16
VT_HEAD_ROWS = HEAD_DIM + BF16_SUBLANES
VMEM_LIMIT = 56 * 1024 * 1024
NEG_BIG = -1e30

_BF16 = jnp.bfloat16
_F32 = jnp.float32


def _resident(shape):
    return pl.BlockSpec(shape, lambda *_: (0,) * len(shape), pipeline_mode=pl.Buffered(1))


def _layer_norm(z, g, b):
    mu = jnp.mean(z, axis=-1, keepdims=True)
    zc = z - mu
    var = jnp.mean(zc * zc, axis=-1, keepdims=True)
    return zc * lax.rsqrt(var + LN_EPS) * g + b


def _dot(a, b):
    return jnp.dot(a, b, preferred_element_type=_F32)


def _ffn_ln(x, wg_ref, wu_ref, wd_ref, g, b):
    xb = x.astype(_BF16)
    y = None
    for start, width in FFN_CHUNKS:
        gate = _dot(xb, wg_ref[:, start:start + width])
        up = _dot(xb, wu_ref[:, start:start + width])
        h = (jax.nn.silu(gate) * up).astype(_BF16)
        part = _dot(h, wd_ref[start:start + width, :])
        y = part if y is None else y + part
    return _layer_norm(DEEPNORM_ALPHA * x + 0.5 * y, g, b)


def _ffn1_kernel(x_ref, wg_ref, wu_ref, wd_ref, g_ref, b_ref, o_ref):
    o_ref[...] = _ffn_ln(x_ref[...], wg_ref, wu_ref, wd_ref, g_ref[...], b_ref[...])


def _ffn1_call(x, wg, wu, wd, g, b):
    S = x.shape[0]
    row = pl.BlockSpec((FFN_ROWS, D_MODEL), lambda i: (i, 0))
    return pl.pallas_call(
        _ffn1_kernel,
        grid=(S // FFN_ROWS,),
        in_specs=[row, _resident(wg.shape), _resident(wu.shape), _resident(wd.shape),
                  _resident(g.shape), _resident(b.shape)],
        out_specs=row,
        out_shape=jax.ShapeDtypeStruct((S, D_MODEL), _F32),
        compiler_params=pltpu.CompilerParams(
            dimension_semantics=("arbitrary",), vmem_limit_bytes=VMEM_LIMIT),
        name="ffn1_ln1",
    )(x, wg, wu, wd, g, b)


def _rotary(t, cos_t, sin_lo, sin_hi):
    half = ROT_DIM // 2
    cols = []
    for j in range(D_MODEL // LANES):
        tj = t[:, j * LANES:(j + 1) * LANES]
        fwd = pltpu.roll(tj, LANES - half, axis=1)
        bwd = pltpu.roll(tj, half, axis=1)
        cols.append(tj * cos_t + fwd * sin_lo + bwd * sin_hi)
    return jnp.concatenate(cols, axis=1)


def _proj_kernel(x_ref, win_ref, lng_ref, lnb_ref, ws_ref, bs_ref, wgo_ref,
                 cos_ref, slo_ref, shi_ref,
                 yg_ref, q_ref, k_ref, vt_ref, kmean_ref, sga_ref):
    xb = x_ref[...].astype(_BF16)

    def proj(idx):
        return _dot(xb, win_ref[:, idx * D_MODEL:(idx + 1) * D_MODEL])

    u = jax.nn.gelu(proj(0))
    vn = _layer_norm(jax.nn.gelu(proj(1)), lng_ref[...], lnb_ref[...]).astype(_BF16)
    tri = (lax.broadcasted_iota(jnp.int32, (GM_CHUNK, GM_CHUNK), 1)
           <= lax.broadcasted_iota(jnp.int32, (GM_CHUNK, GM_CHUNK), 0))
    ws = [jnp.where(tri, ws_ref[g], 0.0).astype(_BF16) for g in range(GM_GROUPS)]
    rows = []
    for c in range(PROJ_ROWS // GM_CHUNK):
        r0 = c * GM_CHUNK
        groups = [_dot(ws[g], vn[r0:r0 + GM_CHUNK, g * GM_GROUP_DIM:(g + 1) * GM_GROUP_DIM])
                  for g in range(GM_GROUPS)]
        rows.append(jnp.concatenate(groups, axis=1) + bs_ref[...])
    sv = jnp.concatenate(rows, axis=0)
    y_gm = _dot((u * sv).astype(_BF16), wgo_ref[...])
    yg_ref[...] = jax.nn.sigmoid(proj(5)) * y_gm
    sga_ref[...] = jax.nn.sigmoid(proj(6))

    cos_t, sin_lo, sin_hi = cos_ref[...], slo_ref[...], shi_ref[...]
    q = _rotary(proj(2), cos_t, sin_lo, sin_hi)
    q_ref[...] = (q * Q_SCALE).astype(_BF16)
    k = _rotary(proj(3), cos_t, sin_lo, sin_hi)
    k_ref[...] = k.astype(_BF16)
    kmean_ref[0] = jnp.mean(k, axis=0, keepdims=True)
    v_t = proj(4).T.astype(_BF16)
    ones = jnp.ones((VT_HEAD_ROWS - HEAD_DIM, PROJ_ROWS), _BF16)
    for h in range(N_HEADS):
        vt_ref[h * VT_HEAD_ROWS:h * VT_HEAD_ROWS + HEAD_DIM, :] = v_t[h * HEAD_DIM:(h + 1) * HEAD_DIM, :]
        vt_ref[h * VT_HEAD_ROWS + HEAD_DIM:(h + 1) * VT_HEAD_ROWS, :] = ones


def _rope_tables(S):
    inv_freq = ROPE_THETA ** (-jnp.arange(0, ROT_DIM, 2, dtype=_F32) / ROT_DIM)
    ang = jnp.arange(S, dtype=jnp.int32).astype(_F32)[:, None] * inv_freq[None, :]
    cos, sin = jnp.cos(ang), jnp.sin(ang)
    half = ROT_DIM // 2
    ones = jnp.ones((S, HEAD_DIM - ROT_DIM), _F32)
    zeros_h = jnp.zeros((S, half), _F32)
    zeros_r = jnp.zeros((S, HEAD_DIM - ROT_DIM), _F32)
    cos_h = jnp.concatenate([cos, cos, ones], axis=1)
    lo_h = jnp.concatenate([-sin, zeros_h, zeros_r], axis=1)
    hi_h = jnp.concatenate([zeros_h, sin, zeros_r], axis=1)
    tile = lambda t: jnp.tile(t, (1, LANES // HEAD_DIM))
    return tile(cos_h), tile(lo_h), tile(hi_h)


def _proj_call(x1, win, lng, lnb, ws, bs_full, wgo):
    S = x1.shape[0]
    nb = S // MOBA_BLOCK
    cos_t, sin_lo, sin_hi = _rope_tables(S)
    row = pl.BlockSpec((PROJ_ROWS, D_MODEL), lambda i: (i, 0))
    tab = pl.BlockSpec((PROJ_ROWS, LANES), lambda i: (i, 0))
    out_shape = (
        jax.ShapeDtypeStruct((S, D_MODEL), _F32),
        jax.ShapeDtypeStruct((S, D_MODEL), _BF16),
        jax.ShapeDtypeStruct((S, D_MODEL), _BF16),
        jax.ShapeDtypeStruct((N_HEADS * VT_HEAD_ROWS, S), _BF16),
        jax.ShapeDtypeStruct((nb, 1, D_MODEL), _F32),
        jax.ShapeDtypeStruct((S, D_MODEL), _F32),
    )
    out_specs = (
        row, row, row,
        pl.BlockSpec((N_HEADS * VT_HEAD_ROWS, PROJ_ROWS), lambda i: (0, i)),
        pl.BlockSpec((1, 1, D_MODEL), lambda i: (i, 0, 0)),
        row,
    )
    return pl.pallas_call(
        _proj_kernel,
        grid=(S // PROJ_ROWS,),
        in_specs=[row, _resident(win.shape), _resident(lng.shape), _resident(lnb.shape),
                  _resident(ws.shape), _resident(bs_full.shape), _resident(wgo.shape),
                  tab, tab, tab],
        out_specs=out_specs,
        out_shape=out_shape,
        compiler_params=pltpu.CompilerParams(
            dimension_semantics=("arbitrary",), vmem_limit_bytes=VMEM_LIMIT),
        name="inproj_gmlp",
    )(x1, win, lng, lnb, ws, bs_full, wgo, cos_t, sin_lo, sin_hi)


def _nt_dot(a, b):
    return lax.dot_general(a, b, (((1,), (1,)), ((), ())), preferred_element_type=_F32)


def _col_max(s):
    return jnp.max(s, axis=0, keepdims=True)


def _moba_kernel(q_ref, k_ref, hot_ref, causal_ref, vt_ref, kmean_ref, o_ref,
                 qz_ref, bias_ref, s_ref, smax_ref, m_ref, acc_ref):
    own = pl.program_id(1)
    nb = kmean_ref.shape[0]
    blk, group = MOBA_BLOCK, MOBA_GROUP
    span = group * blk
    n_full = own // group

    blk_iota = lax.broadcasted_iota(jnp.int32, (nb, blk), 0)
    lane_head = lax.broadcasted_iota(jnp.int32, (blk, LANES), 1) // HEAD_DIM

    def pair_lanes(pair):
        return slice(pair * LANES, (pair + 1) * LANES)

    def keys_with_block_id(pair, start):
        return jnp.concatenate([k_ref[pl.ds(start, span), pair_lanes(pair)], hot_ref[...]], axis=1)

    def queries_with_mask(g, h):
        mask = pltpu.roll(bias_ref[h], (LANES - group * g) % LANES, axis=1)
        return jnp.concatenate([qz_ref[h], mask.astype(_BF16)], axis=1)

    def v_rows(h, start, rows):
        return vt_ref[h * VT_HEAD_ROWS:(h + 1) * VT_HEAD_ROWS, pl.ds(start, rows)]

    kmean = kmean_ref[...]
    km_hi = kmean.astype(_BF16)
    km_rest = kmean - km_hi.astype(_F32)
    km_mid = km_rest.astype(_BF16)
    km_lo = (km_rest - km_mid.astype(_F32)).astype(_BF16)
    for h in range(HEADS_PER_STEP):
        pair = h // HEADS_PER_PAIR
        q_pair = q_ref[:, pair_lanes(pair)]
        km_terms = jnp.concatenate([t[:, pair_lanes(pair)] for t in (km_hi, km_mid, km_lo)],
                                   axis=1)
        qz = jnp.where(lane_head == h % HEADS_PER_PAIR, q_pair, jnp.zeros_like(q_pair))

        gate = _nt_dot(km_terms, jnp.concatenate([qz, qz, qz], axis=1))
        candidate = blk_iota < own
        gate = jnp.where(candidate, gate, -jnp.inf)
        chosen = jnp.zeros((nb, blk), dtype=jnp.bool_)
        for _ in range(MOBA_TOPK):
            best = _col_max(gate)
            first = jnp.min(jnp.where(gate == best, blk_iota, nb), axis=0, keepdims=True)
            pick = blk_iota == first
            chosen = jnp.logical_or(chosen, pick)
            gate = jnp.where(pick, -jnp.inf, gate)
        chosen = jnp.logical_or(jnp.logical_and(chosen, candidate), blk_iota == own)
        bias = jnp.concatenate([jnp.where(chosen, 0.0, NEG_BIG),
                                jnp.zeros((LANES - nb, blk), _F32)], axis=0)
        qz_ref[h] = qz
        bias_ref[h] = bias.T
        m_ref[h] = jnp.full((1, blk), -jnp.inf, _F32)
        acc_ref[h] = jnp.zeros((VT_HEAD_ROWS, blk), _F32)

    last = HEADS_PER_STEP - 1

    def score_head(g, h):
        start = pl.multiple_of(g * span, span)
        s = _nt_dot(keys_with_block_id(h // HEADS_PER_PAIR, start), queries_with_mask(g, h))
        s_ref[h] = s
        smax_ref[h] = _col_max(s)

    def attend_head(g, h, s, s_max):
        m = m_ref[h]
        m_new = jnp.maximum(m, s_max)
        p = jnp.exp2((s - m_new).astype(_BF16))
        v_grp = v_rows(h, pl.multiple_of(g * span, span), span)
        acc_ref[h] = jnp.exp2(m - m_new) * acc_ref[h] + _dot(v_grp, p)
        m_ref[h] = m_new

    for h in range(last):
        score_head(0, h)

    def body(g, carry):
        score_head(g, last)
        for h in range(HEADS_PER_STEP):
            attend_head(g, h, s_ref[h], smax_ref[h])
            if h < last:
                score_head(g + 1, h)
        return carry

    lax.fori_loop(0, n_full, body, 0)

    score_head(n_full, last)
    own_in_group = own - n_full * group
    causal_rows = pl.ds(pl.multiple_of((group - 1 - own_in_group) * blk, blk), span)
    for h in range(HEADS_PER_STEP):
        s = s_ref[h] + causal_ref[causal_rows, :]
        attend_head(n_full, h, s, _col_max(s))
    out_t = jnp.concatenate(
        [acc_ref[h, :HEAD_DIM, :] / acc_ref[h, HEAD_DIM:HEAD_DIM + 1, :]
         for h in range(HEADS_PER_STEP)], axis=0)
    o_ref[...] = out_t.T.astype(o_ref.dtype)


def _moba_call(q, k, vt, kmean):
    S = q.shape[0]
    nb = S // MOBA_BLOCK
    assert nb <= LANES
    n_steps = D_MODEL // STEP_LANES
    span = MOBA_GROUP * MOBA_BLOCK
    hot = (jnp.arange(span, dtype=jnp.int32)[:, None] // MOBA_BLOCK
           == jnp.arange(LANES, dtype=jnp.int32)[None, :]).astype(_BF16)
    rel = jnp.arange((2 * MOBA_GROUP - 1) * MOBA_BLOCK, dtype=jnp.int32) - (MOBA_GROUP - 1) * MOBA_BLOCK
    causal = jnp.where(rel[:, None] <= jnp.arange(MOBA_BLOCK, dtype=jnp.int32)[None, :],
                       0.0, NEG_BIG).astype(_F32)
    step_rows = HEADS_PER_STEP * VT_HEAD_ROWS
    once = pl.Buffered(1)
    return pl.pallas_call(
        _moba_kernel,
        grid=(n_steps, nb),
        in_specs=[
            pl.BlockSpec((MOBA_BLOCK, STEP_LANES), lambda hg, i: (i, hg)),
            pl.BlockSpec((S, STEP_LANES), lambda hg, i: (0, hg), pipeline_mode=once),
            pl.BlockSpec(hot.shape, lambda hg, i: (0, 0), pipeline_mode=once),
            pl.BlockSpec(causal.shape, lambda hg, i: (0, 0), pipeline_mode=once),
            pl.BlockSpec((step_rows, S), lambda hg, i: (hg, 0), pipeline_mode=once),
            pl.BlockSpec((nb, STEP_LANES), lambda hg, i: (0, hg)),
        ],
        out_specs=pl.BlockSpec((MOBA_BLOCK, STEP_LANES), lambda hg, i: (i, hg)),
        out_shape=jax.ShapeDtypeStruct((S, D_MODEL), _BF16),
        scratch_shapes=[
            pltpu.VMEM((HEADS_PER_STEP, MOBA_BLOCK, LANES), _BF16),
            pltpu.VMEM((HEADS_PER_STEP, MOBA_BLOCK, LANES), _F32),
            pltpu.VMEM((HEADS_PER_STEP, MOBA_GROUP * MOBA_BLOCK, MOBA_BLOCK), _F32),
            pltpu.VMEM((HEADS_PER_STEP, 1, MOBA_BLOCK), _F32),
            pltpu.VMEM((HEADS_PER_STEP, 1, MOBA_BLOCK), _F32),
            pltpu.VMEM((HEADS_PER_STEP, VT_HEAD_ROWS, MOBA_BLOCK), _F32),
        ],
        compiler_params=pltpu.CompilerParams(
            dimension_semantics=("arbitrary", "arbitrary"), vmem_limit_bytes=VMEM_LIMIT),
        name="moba_attention",
    )(q, k, hot, causal, vt, kmean)


def _out_kernel(x1_ref, yatt_ref, yg_ref, sga_ref, wao_ref, wo_ref, g2_ref, b2_ref,
                wg_ref, wu_ref, wd_ref, g3_ref, b3_ref, o_ref):
    y_att = _dot(yatt_ref[...], wao_ref[...])
    merged = (yg_ref[...] + sga_ref[...] * y_att).astype(_BF16)
    z = DEEPNORM_ALPHA * x1_ref[...] + _dot(merged, wo_ref[...])
    x2 = _layer_norm(z, g2_ref[...], b2_ref[...])
    o_ref[...] = _ffn_ln(x2, wg_ref, wu_ref, wd_ref, g3_ref[...], b3_ref[...])


def _out_call(x1, yatt, yg, sga, wao, wo, g2, b2, wg, wu, wd, g3, b3):
    S = x1.shape[0]
    row = pl.BlockSpec((FFN_ROWS, D_MODEL), lambda i: (i, 0))
    weights = (wao, wo, g2, b2, wg, wu, wd, g3, b3)
    return pl.pallas_call(
        _out_kernel,
        grid=(S // FFN_ROWS,),
        in_specs=[row, row, row, row] + [_resident(w.shape) for w in weights],
        out_specs=row,
        out_shape=jax.ShapeDtypeStruct((S, D_MODEL), _F32),
        compiler_params=pltpu.CompilerParams(
            dimension_semantics=("arbitrary",), vmem_limit_bytes=VMEM_LIMIT),
        name="merge_ffn2",
    )(x1, yatt, yg, sga, *weights)


def kernel(x, ffn1_w_gate, ffn1_w_up, ffn1_w_down, ln1_g, ln1_b, w_in, gm_ln_g, gm_ln_b, gm_w_s, gm_b_s, w_gm_out, w_att_out, w_o, ln2_g, ln2_b, ffn2_w_gate, ffn2_w_up, ffn2_w_down, ln3_g, ln3_b):
    B, S, D = x.shape
    assert D == D_MODEL and S % FFN_ROWS == 0 and S % (MOBA_BLOCK * MOBA_GROUP) == 0
    bf = lambda w: w.astype(_BF16)
    outs = []
    for b in range(B):
        xb = x[b]
        for l in range(DEPTH):
            x1 = _ffn1_call(xb, bf(ffn1_w_gate[l]), bf(ffn1_w_up[l]), bf(ffn1_w_down[l]),
                            ln1_g[l][None], ln1_b[l][None])
            bs_full = jnp.repeat(gm_b_s[l].T, GM_GROUP_DIM, axis=1)
            yg, q, k, vt, kmean, sga = _proj_call(
                x1, bf(w_in[l]), gm_ln_g[l][None], gm_ln_b[l][None], gm_w_s[l], bs_full,
                bf(w_gm_out[l]))
            yatt = _moba_call(q, k, vt, kmean.reshape(S // MOBA_BLOCK, D_MODEL))
            xb = _out_call(x1, yatt, yg, sga, bf(w_att_out[l]), bf(w_o[l]),
                           ln2_g[l][None], ln2_b[l][None],
                           bf(ffn2_w_gate[l]), bf(ffn2_w_up[l]), bf(ffn2_w_down[l]),
                           ln3_g[l][None], ln3_b[l][None])
        outs.append(xb)
    return jnp.stack(outs, axis=0)
```

```python
import functools

import jax
import jax.numpy as jnp
import numpy as np
from jax import lax
from jax.experimental import pallas as pl
from jax.experimental.pallas import tpu as pltpu

D_MODEL = 1024
DEPTH = 1
GM_GROUPS = 8
GM_CHUNK = 128
GM_GROUP_DIM = D_MODEL // GM_GROUPS
N_HEADS = 16
HEAD_DIM = 64
MOBA_BLOCK = 256
MOBA_TOPK = 3
ROPE_THETA = 500000.0
ROT_DIM = HEAD_DIM // 4
D_FF = 2816
LN_EPS = 1e-5
DEEPNORM_ALPHA = (2.0 * DEPTH) ** 0.25
ATT_SCALE = HEAD_DIM ** -0.5
LOG2_E = 1.4426950408889634
Q_SCALE = ATT_SCALE * LOG2_E

LANES = 128
FFN_ROWS = 512
FFN_CHUNKS = ((0, 1024), (1024, 1024), (2048, 768))
PROJ_ROWS = 256
HEADS_PER_PAIR = LANES // HEAD_DIM
PAIRS_PER_STEP = 4
HEADS_PER_STEP = HEADS_PER_PAIR * PAIRS_PER_STEP
STEP_LANES = PAIRS_PER_STEP * LANES
MOBA_GROUP = 4
BF16_SUBLANES = 16
VT_HEAD_ROWS = HEAD_DIM + BF16_SUBLANES
VMEM_LIMIT = 56 * 1024 * 1024
NEG_BIG = -1e30

_BF16 = jnp.bfloat16
_F32 = jnp.float32


def _resident(shape):
    return pl.BlockSpec(shape, lambda *_: (0,) * len(shape), pipeline_mode=pl.Buffered(1))


def _layer_norm(z, g, b):
    mu = jnp.mean(z, axis=-1, keepdims=True)
    zc = z - mu
    var = jnp.mean(zc * zc, axis=-1, keepdims=True)
    return zc * lax.rsqrt(var + LN_EPS) * g + b


def _dot(a, b):
    return jnp.dot(a, b, preferred_element_type=_F32)


def _ffn_ln(x, wg_ref, wu_ref, wd_ref, g, b):
    xb = x.astype(_BF16)
    y = None
    for start, width in FFN_CHUNKS:
        gate = _dot(xb, wg_ref[:, start:start + width])
        up = _dot(xb, wu_ref[:, start:start + width])
        h = (jax.nn.silu(gate) * up).astype(_BF16)
        part = _dot(h, wd_ref[start:start + width, :])
        y = part if y is None else y + part
    return _layer_norm(DEEPNORM_ALPHA * x + 0.5 * y, g, b)


def _ffn1_kernel(x_ref, wg_ref, wu_ref, wd_ref, g_ref, b_ref, o_ref):
    o_ref[...] = _ffn_ln(x_ref[...], wg_ref, wu_ref, wd_ref, g_ref[...], b_ref[...])


def _ffn1_call(x, wg, wu, wd, g, b):
    S = x.shape[0]
    row = pl.BlockSpec((FFN_ROWS, D_MODEL), lambda i: (i, 0))
    return pl.pallas_call(
        _ffn1_kernel,
        grid=(S // FFN_ROWS,),
        in_specs=[row, _resident(wg.shape), _resident(wu.shape), _resident(wd.shape),
                  _resident(g.shape), _resident(b.shape)],
        out_specs=row,
        out_shape=jax.ShapeDtypeStruct((S, D_MODEL), _F32),
        compiler_params=pltpu.CompilerParams(
            dimension_semantics=("arbitrary",), vmem_limit_bytes=VMEM_LIMIT),
        name="ffn1_ln1",
    )(x, wg, wu, wd, g, b)


def _rotary(t, cos_t, sin_lo, sin_hi):
    half = ROT_DIM // 2
    cols = []
    for j in range(D_MODEL // LANES):
        tj = t[:, j * LANES:(j + 1) * LANES]
        fwd = pltpu.roll(tj, LANES - half, axis=1)
        bwd = pltpu.roll(tj, half, axis=1)
        cols.append(tj * cos_t + fwd * sin_lo + bwd * sin_hi)
    return jnp.concatenate(cols, axis=1)


def _proj_kernel(x_ref, win_ref, lng_ref, lnb_ref, ws_ref, bs_ref, wgo_ref,
                 cos_ref, slo_ref, shi_ref,
                 yg_ref, q_ref, k_ref, vt_ref, kmean_ref, sga_ref):
    xb = x_ref[...].astype(_BF16)

    def proj(idx):
        return _dot(xb, win_ref[:, idx * D_MODEL:(idx + 1) * D_MODEL])

    u = jax.nn.gelu(proj(0))
    vn = _layer_norm(jax.nn.gelu(proj(1)), lng_ref[...], lnb_ref[...]).astype(_BF16)
    tri = (lax.broadcasted_iota(jnp.int32, (GM_CHUNK, GM_CHUNK), 1)
           <= lax.broadcasted_iota(jnp.int32, (GM_CHUNK, GM_CHUNK), 0))
    ws = [jnp.where(tri, ws_ref[g], 0.0).astype(_BF16) for g in range(GM_GROUPS)]
    rows = []
    for c in range(PROJ_ROWS // GM_CHUNK):
        r0 = c * GM_CHUNK
        groups = [_dot(ws[g], vn[r0:r0 + GM_CHUNK, g * GM_GROUP_DIM:(g + 1) * GM_GROUP_DIM])
                  for g in range(GM_GROUPS)]
        rows.append(jnp.concatenate(groups, axis=1) + bs_ref[...])
    sv = jnp.concatenate(rows, axis=0)
    y_gm = _dot((u * sv).astype(_BF16), wgo_ref[...])
    yg_ref[...] = jax.nn.sigmoid(proj(5)) * y_gm
    sga_ref[...] = jax.nn.sigmoid(proj(6))

    cos_t, sin_lo, sin_hi = cos_ref[...], slo_ref[...], shi_ref[...]
    q = _rotary(proj(2), cos_t, sin_lo, sin_hi)
    q_ref[...] = (q * Q_SCALE).astype(_BF16)
    k = _rotary(proj(3), cos_t, sin_lo, sin_hi)
    k_ref[...] = k.astype(_BF16)
    kmean_ref[0] = jnp.mean(k, axis=0, keepdims=True)
    v_t = proj(4).T.astype(_BF16)
    ones = jnp.ones((VT_HEAD_ROWS - HEAD_DIM, PROJ_ROWS), _BF16)
    for h in range(N_HEADS):
        vt_ref[h * VT_HEAD_ROWS:h * VT_HEAD_ROWS + HEAD_DIM, :] = v_t[h * HEAD_DIM:(h + 1) * HEAD_DIM, :]
        vt_ref[h * VT_HEAD_ROWS + HEAD_DIM:(h + 1) * VT_HEAD_ROWS, :] = ones


def _rope_tables(S):
    half = ROT_DIM // 2
    d = jnp.arange(LANES, dtype=jnp.int32) % HEAD_DIM
    inv_freq = ROPE_THETA ** (-(2 * (d % half)).astype(_F32) / ROT_DIM)
    ang = jnp.arange(S, dtype=jnp.int32).astype(_F32)[:, None] * inv_freq[None, :]
    cos, sin = jnp.cos(ang), jnp.sin(ang)
    cos_t = jnp.where(d < ROT_DIM, cos, 1.0)
    sin_lo = jnp.where(d < half, -sin, 0.0)
    sin_hi = jnp.where((d >= half) & (d < ROT_DIM), sin, 0.0)
    return cos_t, sin_lo, sin_hi


def _proj_call(x1, win, lng, lnb, ws, bs_full, wgo):
    S = x1.shape[0]
    nb = S // MOBA_BLOCK
    cos_t, sin_lo, sin_hi = _rope_tables(S)
    row = pl.BlockSpec((PROJ_ROWS, D_MODEL), lambda i: (i, 0))
    tab = pl.BlockSpec((PROJ_ROWS, LANES), lambda i: (i, 0))
    out_shape = (
        jax.ShapeDtypeStruct((S, D_MODEL), _F32),
        jax.ShapeDtypeStruct((S, D_MODEL), _BF16),
        jax.ShapeDtypeStruct((S, D_MODEL), _BF16),
        jax.ShapeDtypeStruct((N_HEADS * VT_HEAD_ROWS, S), _BF16),
        jax.ShapeDtypeStruct((nb, 1, D_MODEL), _F32),
        jax.ShapeDtypeStruct((S, D_MODEL), _F32),
    )
    out_specs = (
        row, row, row,
        pl.BlockSpec((N_HEADS * VT_HEAD_ROWS, PROJ_ROWS), lambda i: (0, i)),
        pl.BlockSpec((1, 1, D_MODEL), lambda i: (i, 0, 0)),
        row,
    )
    return pl.pallas_call(
        _proj_kernel,
        grid=(S // PROJ_ROWS,),
        in_specs=[row, _resident(win.shape), _resident(lng.shape), _resident(lnb.shape),
                  _resident(ws.shape), _resident(bs_full.shape), _resident(wgo.shape),
                  tab, tab, tab],
        out_specs=out_specs,
        out_shape=out_shape,
        compiler_params=pltpu.CompilerParams(
            dimension_semantics=("arbitrary",), vmem_limit_bytes=VMEM_LIMIT),
        name="inproj_gmlp",
    )(x1, win, lng, lnb, ws, bs_full, wgo, cos_t, sin_lo, sin_hi)


def _nt_dot(a, b):
    return lax.dot_general(a, b, (((1,), (1,)), ((), ())), preferred_element_type=_F32)


def _col_max(s):
    return jnp.max(s, axis=0, keepdims=True)


def _moba_kernel(q_ref, k_ref, hot_ref, causal_ref, vt_ref, kmean_ref, o_ref,
                 qz_ref, bias_ref, s_ref, smax_ref, m_ref, acc_ref):
    own = pl.program_id(1)
    nb = kmean_ref.shape[0]
    blk, group = MOBA_BLOCK, MOBA_GROUP
    span = group * blk
    n_full = own // group

    blk_iota = lax.broadcasted_iota(jnp.int32, (nb, blk), 0)
    lane_head = lax.broadcasted_iota(jnp.int32, (blk, LANES), 1) // HEAD_DIM

    def pair_lanes(pair):
        return slice(pair * LANES, (pair + 1) * LANES)

    def keys_with_block_id(pair, start):
        return jnp.concatenate([k_ref[pl.ds(start, span), pair_lanes(pair)], hot_ref[...]], axis=1)

    def queries_with_mask(g, h):
        mask = pltpu.roll(bias_ref[h], (LANES - group * g) % LANES, axis=1)
        return jnp.concatenate([qz_ref[h], mask.astype(_BF16)], axis=1)

    def v_rows(h, start, rows):
        return vt_ref[h * VT_HEAD_ROWS:(h + 1) * VT_HEAD_ROWS, pl.ds(start, rows)]

    kmean = kmean_ref[...]
    km_hi = kmean.astype(_BF16)
    km_rest = kmean - km_hi.astype(_F32)
    km_mid = km_rest.astype(_BF16)
    km_lo = (km_rest - km_mid.astype(_F32)).astype(_BF16)
    for h in range(HEADS_PER_STEP):
        pair = h // HEADS_PER_PAIR
        q_pair = q_ref[:, pair_lanes(pair)]
        km_terms = jnp.concatenate([t[:, pair_lanes(pair)] for t in (km_hi, km_mid, km_lo)],
                                   axis=1)
        qz = jnp.where(lane_head == h % HEADS_PER_PAIR, q_pair, jnp.zeros_like(q_pair))

        gate = _nt_dot(km_terms, jnp.concatenate([qz, qz, qz], axis=1))
        candidate = blk_iota < own
        gate = jnp.where(candidate, gate, -jnp.inf)
        chosen = jnp.zeros((nb, blk), dtype=jnp.bool_)
        for _ in range(MOBA_TOPK):
            best = _col_max(gate)
            first = jnp.min(jnp.where(gate == best, blk_iota, nb), axis=0, keepdims=True)
            pick = blk_iota == first
            chosen = jnp.logical_or(chosen, pick)
            gate = jnp.where(pick, -jnp.inf, gate)
        chosen = jnp.logical_or(jnp.logical_and(chosen, candidate), blk_iota == own)
        bias = jnp.concatenate([jnp.where(chosen, 0.0, NEG_BIG),
                                jnp.zeros((LANES - nb, blk), _F32)], axis=0)
        qz_ref[h] = qz
        bias_ref[h] = bias.T
        m_ref[h] = jnp.full((1, blk), -jnp.inf, _F32)
        acc_ref[h] = jnp.zeros((VT_HEAD_ROWS, blk), _F32)

    last = HEADS_PER_STEP - 1

    def score_head(g, h):
        start = pl.multiple_of(g * span, span)
        s = _nt_dot(keys_with_block_id(h // HEADS_PER_PAIR, start), queries_with_mask(g, h))
        s_ref[h] = s
        smax_ref[h] = _col_max(s)

    def attend_head(g, h, s, s_max):
        m = m_ref[h]
        m_new = jnp.maximum(m, s_max)
        p = jnp.exp2((s - m_new).astype(_BF16))
        v_grp = v_rows(h, pl.multiple_of(g * span, span), span)
        acc_ref[h] = jnp.exp2(m - m_new) * acc_ref[h] + _dot(v_grp, p)
        m_ref[h] = m_new

    for h in range(last):
        score_head(0, h)

    def body(g, carry):
        score_head(g, last)
        for h in range(HEADS_PER_STEP):
            attend_head(g, h, s_ref[h], smax_ref[h])
            if h < last:
                score_head(g + 1, h)
        return carry

    lax.fori_loop(0, n_full, body, 0)

    score_head(n_full, last)
    own_in_group = own - n_full * group
    causal_rows = pl.ds(pl.multiple_of((group - 1 - own_in_group) * blk, blk), span)
    for h in range(HEADS_PER_STEP):
        s = s_ref[h] + causal_ref[causal_rows, :]
        attend_head(n_full, h, s, _col_max(s))
    out_t = jnp.concatenate(
        [acc_ref[h, :HEAD_DIM, :] / acc_ref[h, HEAD_DIM:HEAD_DIM + 1, :]
         for h in range(HEADS_PER_STEP)], axis=0)
    o_ref[...] = out_t.T.astype(o_ref.dtype)


def _moba_call(q, k, vt, kmean):
    S = q.shape[0]
    nb = S // MOBA_BLOCK
    assert nb <= LANES
    n_steps = D_MODEL // STEP_LANES
    span = MOBA_GROUP * MOBA_BLOCK
    hot = (jnp.arange(span, dtype=jnp.int32)[:, None] // MOBA_BLOCK
           == jnp.arange(LANES, dtype=jnp.int32)[None, :]).astype(_BF16)
    rel = jnp.arange((2 * MOBA_GROUP - 1) * MOBA_BLOCK, dtype=jnp.int32) - (MOBA_GROUP - 1) * MOBA_BLOCK
    causal = jnp.where(rel[:, None] <= jnp.arange(MOBA_BLOCK, dtype=jnp.int32)[None, :],
                       0.0, NEG_BIG).astype(_F32)
    step_rows = HEADS_PER_STEP * VT_HEAD_ROWS
    once = pl.Buffered(1)
    return pl.pallas_call(
        _moba_kernel,
        grid=(n_steps, nb),
        in_specs=[
            pl.BlockSpec((MOBA_BLOCK, STEP_LANES), lambda hg, i: (i, hg)),
            pl.BlockSpec((S, STEP_LANES), lambda hg, i: (0, hg), pipeline_mode=once),
            pl.BlockSpec(hot.shape, lambda hg, i: (0, 0), pipeline_mode=once),
            pl.BlockSpec(causal.shape, lambda hg, i: (0, 0), pipeline_mode=once),
            pl.BlockSpec((step_rows, S), lambda hg, i: (hg, 0), pipeline_mode=once),
            pl.BlockSpec((nb, STEP_LANES), lambda hg, i: (0, hg)),
        ],
        out_specs=pl.BlockSpec((MOBA_BLOCK, STEP_LANES), lambda hg, i: (i, hg)),
        out_shape=jax.ShapeDtypeStruct((S, D_MODEL), _BF16),
        scratch_shapes=[
            pltpu.VMEM((HEADS_PER_STEP, MOBA_BLOCK, LANES), _BF16),
            pltpu.VMEM((HEADS_PER_STEP, MOBA_BLOCK, LANES), _F32),
            pltpu.VMEM((HEADS_PER_STEP, MOBA_GROUP * MOBA_BLOCK, MOBA_BLOCK), _F32),
            pltpu.VMEM((HEADS_PER_STEP, 1, MOBA_BLOCK), _F32),
            pltpu.VMEM((HEADS_PER_STEP, 1, MOBA_BLOCK), _F32),
            pltpu.VMEM((HEADS_PER_STEP, VT_HEAD_ROWS, MOBA_BLOCK), _F32),
        ],
        compiler_params=pltpu.CompilerParams(
            dimension_semantics=("arbitrary", "arbitrary"), vmem_limit_bytes=VMEM_LIMIT),
        name="moba_attention",
    )(q, k, hot, causal, vt, kmean)


def _out_kernel(x1_ref, yatt_ref, yg_ref, sga_ref, wao_ref, wo_ref, g2_ref, b2_ref,
                wg_ref, wu_ref, wd_ref, g3_ref, b3_ref, o_ref):
    y_att = _dot(yatt_ref[...], wao_ref[...])
    merged = (yg_ref[...] + sga_ref[...] * y_att).astype(_BF16)
    z = DEEPNORM_ALPHA * x1_ref[...] + _dot(merged, wo_ref[...])
    x2 = _layer_norm(z, g2_ref[...], b2_ref[...])
    o_ref[...] = _ffn_ln(x2, wg_ref, wu_ref, wd_ref, g3_ref[...], b3_ref[...])


def _out_call(x1, yatt, yg, sga, wao, wo, g2, b2, wg, wu, wd, g3, b3):
    S = x1.shape[0]
    row = pl.BlockSpec((FFN_ROWS, D_MODEL), lambda i: (i, 0))
    weights = (wao, wo, g2, b2, wg, wu, wd, g3, b3)
    return pl.pallas_call(
        _out_kernel,
        grid=(S // FFN_ROWS,),
        in_specs=[row, row, row, row] + [_resident(w.shape) for w in weights],
        out_specs=row,
        out_shape=jax.ShapeDtypeStruct((S, D_MODEL), _F32),
        compiler_params=pltpu.CompilerParams(
            dimension_semantics=("arbitrary",), vmem_limit_bytes=VMEM_LIMIT),
        name="merge_ffn2",
    )(x1, yatt, yg, sga, *weights)


def kernel(x, ffn1_w_gate, ffn1_w_up, ffn1_w_down, ln1_g, ln1_b, w_in, gm_ln_g, gm_ln_b, gm_w_s, gm_b_s, w_gm_out, w_att_out, w_o, ln2_g, ln2_b, ffn2_w_gate, ffn2_w_up, ffn2_w_down, ln3_g, ln3_b):
    B, S, D = x.shape
    assert D == D_MODEL and S % FFN_ROWS == 0 and S % (MOBA_BLOCK * MOBA_GROUP) == 0
    bf = lambda w: w.astype(_BF16)
    outs = []
    for b in range(B):
        xb = x[b]
        for l in range(DEPTH):
            x1 = _ffn1_call(xb, bf(ffn1_w_gate[l]), bf(ffn1_w_up[l]), bf(ffn1_w_down[l]),
                            ln1_g[l][None], ln1_b[l][None])
            bs_full = jnp.repeat(gm_b_s[l].T, GM_GROUP_DIM, axis=1)
            yg, q, k, vt, kmean, sga = _proj_call(
                x1, bf(w_in[l]), gm_ln_g[l][None], gm_ln_b[l][None], gm_w_s[l], bs_full,
                bf(w_gm_out[l]))
            yatt = _moba_call(q, k, vt, kmean.reshape(S // MOBA_BLOCK, D_MODEL))
            xb = _out_call(x1, yatt, yg, sga, bf(w_att_out[l]), bf(w_o[l]),
                           ln2_g[l][None], ln2_b[l][None],
                           bf(ffn2_w_gate[l]), bf(ffn2_w_up[l]), bf(ffn2_w_down[l]),
                           ln3_g[l][None], ln3_b[l][None])
        outs.append(xb)
    return jnp.stack(outs, axis=0)
```

```python
import functools

import jax
import jax.numpy as jnp
import numpy as np
from jax import lax
from jax.experimental import pallas as pl
from jax.experimental.pallas import tpu as pltpu

D_MODEL = 1024
DEPTH = 1
GM_GROUPS = 8
GM_CHUNK = 128
GM_GROUP_DIM = D_MODEL // GM_GROUPS
N_HEADS = 16
HEAD_DIM = 64
MOBA_BLOCK = 256
MOBA_TOPK = 3
ROPE_THETA = 500000.0
ROT_DIM = HEAD_DIM // 4
D_FF = 2816
LN_EPS = 1e-5
DEEPNORM_ALPHA = (2.0 * DEPTH) ** 0.25
ATT_SCALE = HEAD_DIM ** -0.5
LOG2_E = 1.4426950408889634
Q_SCALE = ATT_SCALE * LOG2_E

LANES = 128
FFN_ROWS = 512
FFN_CHUNKS = ((0, 1024), (1024, 1024), (2048, 768))
PROJ_ROWS = 256
HEADS_PER_PAIR = LANES // HEAD_DIM
PAIRS_PER_STEP = D_MODEL // LANES
HEADS_PER_STEP = HEADS_PER_PAIR * PAIRS_PER_STEP
STEP_LANES = PAIRS_PER_STEP * LANES
MOBA_GROUP = 4
K_SLOTS = 3
BF16_SUBLANES = 16
VT_HEAD_ROWS = HEAD_DIM + BF16_SUBLANES
VMEM_LIMIT = 56 * 1024 * 1024
NEG_BIG = -1e30

_BF16 = jnp.bfloat16
_F32 = jnp.float32


def _resident(shape):
    return pl.BlockSpec(shape, lambda *_: (0,) * len(shape), pipeline_mode=pl.Buffered(1))


def _layer_norm(z, g, b):
    mu = jnp.mean(z, axis=-1, keepdims=True)
    zc = z - mu
    var = jnp.mean(zc * zc, axis=-1, keepdims=True)
    return zc * lax.rsqrt(var + LN_EPS) * g + b


def _dot(a, b):
    return jnp.dot(a, b, preferred_element_type=_F32)


def _ffn_ln(x, wg_ref, wu_ref, wd_ref, g, b):
    xb = x.astype(_BF16)
    y = None
    for start, width in FFN_CHUNKS:
        gate = _dot(xb, wg_ref[:, start:start + width])
        up = _dot(xb, wu_ref[:, start:start + width])
        h = (jax.nn.silu(gate) * up).astype(_BF16)
        part = _dot(h, wd_ref[start:start + width, :])
        y = part if y is None else y + part
    return _layer_norm(DEEPNORM_ALPHA * x + 0.5 * y, g, b)


def _ffn1_kernel(x_ref, wg_ref, wu_ref, wd_ref, g_ref, b_ref, o_ref):
    o_ref[...] = _ffn_ln(x_ref[...], wg_ref, wu_ref, wd_ref, g_ref[...], b_ref[...])


def _ffn1_call(x, wg, wu, wd, g, b):
    S = x.shape[0]
    row = pl.BlockSpec((FFN_ROWS, D_MODEL), lambda i: (i, 0))
    return pl.pallas_call(
        _ffn1_kernel,
        grid=(S // FFN_ROWS,),
        in_specs=[row, _resident(wg.shape), _resident(wu.shape), _resident(wd.shape),
                  _resident(g.shape), _resident(b.shape)],
        out_specs=row,
        out_shape=jax.ShapeDtypeStruct((S, D_MODEL), _F32),
        compiler_params=pltpu.CompilerParams(
            dimension_semantics=("arbitrary",), vmem_limit_bytes=VMEM_LIMIT),
        name="ffn1_ln1",
    )(x, wg, wu, wd, g, b)


def _rotary(t, cos_t, sin_lo, sin_hi):
    half = ROT_DIM // 2
    cols = []
    for j in range(D_MODEL // LANES):
        tj = t[:, j * LANES:(j + 1) * LANES]
        fwd = pltpu.roll(tj, LANES - half, axis=1)
        bwd = pltpu.roll(tj, half, axis=1)
        cols.append(tj * cos_t + fwd * sin_lo + bwd * sin_hi)
    return jnp.concatenate(cols, axis=1)


def _proj_kernel(x_ref, win_ref, lng_ref, lnb_ref, ws_ref, bs_ref, wgo_ref,
                 cos_ref, slo_ref, shi_ref,
                 yg_ref, q_ref, k_ref, vt_ref, kmean_ref, sga_ref):
    xb = x_ref[...].astype(_BF16)

    def proj(idx):
        return _dot(xb, win_ref[:, idx * D_MODEL:(idx + 1) * D_MODEL])

    u = jax.nn.gelu(proj(0))
    vn = _layer_norm(jax.nn.gelu(proj(1)), lng_ref[...], lnb_ref[...]).astype(_BF16)
    tri = (lax.broadcasted_iota(jnp.int32, (GM_CHUNK, GM_CHUNK), 1)
           <= lax.broadcasted_iota(jnp.int32, (GM_CHUNK, GM_CHUNK), 0))
    ws = [jnp.where(tri, ws_ref[g], 0.0).astype(_BF16) for g in range(GM_GROUPS)]
    rows = []
    for c in range(PROJ_ROWS // GM_CHUNK):
        r0 = c * GM_CHUNK
        groups = [_dot(ws[g], vn[r0:r0 + GM_CHUNK, g * GM_GROUP_DIM:(g + 1) * GM_GROUP_DIM])
                  for g in range(GM_GROUPS)]
        rows.append(jnp.concatenate(groups, axis=1) + bs_ref[...])
    sv = jnp.concatenate(rows, axis=0)
    y_gm = _dot((u * sv).astype(_BF16), wgo_ref[...])
    yg_ref[...] = jax.nn.sigmoid(proj(5)) * y_gm
    sga_ref[...] = jax.nn.sigmoid(proj(6))

    cos_t, sin_lo, sin_hi = cos_ref[...], slo_ref[...], shi_ref[...]
    q = _rotary(proj(2), cos_t, sin_lo, sin_hi)
    q_ref[...] = (q * Q_SCALE).astype(_BF16)
    k = _rotary(proj(3), cos_t, sin_lo, sin_hi)
    k_ref[...] = k.astype(_BF16)
    kmean_ref[0] = jnp.mean(k, axis=0, keepdims=True)
    v_t = proj(4).T.astype(_BF16)
    ones = jnp.ones((VT_HEAD_ROWS - HEAD_DIM, PROJ_ROWS), _BF16)
    for h in range(N_HEADS):
        vt_ref[h * VT_HEAD_ROWS:h * VT_HEAD_ROWS + HEAD_DIM, :] = v_t[h * HEAD_DIM:(h + 1) * HEAD_DIM, :]
        vt_ref[h * VT_HEAD_ROWS + HEAD_DIM:(h + 1) * VT_HEAD_ROWS, :] = ones


def _rope_tables(S):
    half = ROT_DIM // 2
    d = jnp.arange(LANES, dtype=jnp.int32) % HEAD_DIM
    inv_freq = ROPE_THETA ** (-(2 * (d % half)).astype(_F32) / ROT_DIM)
    ang = jnp.arange(S, dtype=jnp.int32).astype(_F32)[:, None] * inv_freq[None, :]
    cos, sin = jnp.cos(ang), jnp.sin(ang)
    cos_t = jnp.where(d < ROT_DIM, cos, 1.0)
    sin_lo = jnp.where(d < half, -sin, 0.0)
    sin_hi = jnp.where((d >= half) & (d < ROT_DIM), sin, 0.0)
    return cos_t, sin_lo, sin_hi


def _proj_call(x1, win, lng, lnb, ws, bs_full, wgo):
    S = x1.shape[0]
    nb = S // MOBA_BLOCK
    cos_t, sin_lo, sin_hi = _rope_tables(S)
    row = pl.BlockSpec((PROJ_ROWS, D_MODEL), lambda i: (i, 0))
    tab = pl.BlockSpec((PROJ_ROWS, LANES), lambda i: (i, 0))
    out_shape = (
        jax.ShapeDtypeStruct((S, D_MODEL), _F32),
        jax.ShapeDtypeStruct((S, D_MODEL), _BF16),
        jax.ShapeDtypeStruct((S, D_MODEL), _BF16),
        jax.ShapeDtypeStruct((N_HEADS * VT_HEAD_ROWS, S), _BF16),
        jax.ShapeDtypeStruct((nb, 1, D_MODEL), _F32),
        jax.ShapeDtypeStruct((S, D_MODEL), _F32),
    )
    out_specs = (
        row, row, row,
        pl.BlockSpec((N_HEADS * VT_HEAD_ROWS, PROJ_ROWS), lambda i: (0, i)),
        pl.BlockSpec((1, 1, D_MODEL), lambda i: (i, 0, 0)),
        row,
    )
    return pl.pallas_call(
        _proj_kernel,
        grid=(S // PROJ_ROWS,),
        in_specs=[row, _resident(win.shape), _resident(lng.shape), _resident(lnb.shape),
                  _resident(ws.shape), _resident(bs_full.shape), _resident(wgo.shape),
                  tab, tab, tab],
        out_specs=out_specs,
        out_shape=out_shape,
        compiler_params=pltpu.CompilerParams(
            dimension_semantics=("arbitrary",), vmem_limit_bytes=VMEM_LIMIT),
        name="inproj_gmlp",
    )(x1, win, lng, lnb, ws, bs_full, wgo, cos_t, sin_lo, sin_hi)


def _nt_dot(a, b):
    return lax.dot_general(a, b, (((1,), (1,)), ((), ())), preferred_element_type=_F32)


def _col_max(s):
    return jnp.max(s, axis=0, keepdims=True)


def _moba_kernel(q_ref, k_hbm, hot_ref, causal_ref, vt_hbm, kmean_ref, o_ref,
                 qz_ref, bias_ref, s_ref, smax_ref, m_ref, acc_ref, k_buf, v_buf, k_sem, v_sem):
    own = pl.program_id(0)
    nb = kmean_ref.shape[0]
    blk, group = MOBA_BLOCK, MOBA_GROUP
    span = group * blk
    n_full = own // group

    blk_iota = lax.broadcasted_iota(jnp.int32, (nb, blk), 0)
    lane_head = lax.broadcasted_iota(jnp.int32, (blk, LANES), 1) // HEAD_DIM

    def pair_lanes(pair):
        return slice(pair * LANES, (pair + 1) * LANES)

    last_group = nb // group - 1

    def k_copy(g, slot):
        start = pl.multiple_of(jnp.minimum(g, last_group) * span, span)
        return pltpu.make_async_copy(k_hbm.at[pl.ds(start, span), :], k_buf.at[slot], k_sem.at[slot])

    def v_copy(g, slot):
        start = pl.multiple_of(jnp.minimum(g, last_group) * span, span)
        return pltpu.make_async_copy(vt_hbm.at[:, pl.ds(start, span)], v_buf.at[slot], v_sem.at[slot])

    k_copy(0, 0).start()
    v_copy(0, 0).start()
    k_copy(1, 1).start()

    def keys_with_block_id(pair, g):
        return jnp.concatenate([k_buf[g % K_SLOTS, :, pair_lanes(pair)], hot_ref[...]], axis=1)

    def queries_with_mask(g, h):
        mask = pltpu.roll(bias_ref[h], (LANES - group * g) % LANES, axis=1)
        return jnp.concatenate([qz_ref[h], mask.astype(_BF16)], axis=1)

    def v_rows(h, g):
        return v_buf[g % 2, h * VT_HEAD_ROWS:(h + 1) * VT_HEAD_ROWS, :]

    kmean = kmean_ref[...]
    km_hi = kmean.astype(_BF16)
    km_rest = kmean - km_hi.astype(_F32)
    km_mid = km_rest.astype(_BF16)
    km_lo = (km_rest - km_mid.astype(_F32)).astype(_BF16)
    for h in range(HEADS_PER_STEP):
        pair = h // HEADS_PER_PAIR
        q_pair = q_ref[:, pair_lanes(pair)]
        km_terms = jnp.concatenate([t[:, pair_lanes(pair)] for t in (km_hi, km_mid, km_lo)],
                                   axis=1)
        qz = jnp.where(lane_head == h % HEADS_PER_PAIR, q_pair, jnp.zeros_like(q_pair))

        gate = _nt_dot(km_terms, jnp.concatenate([qz, qz, qz], axis=1))
        candidate = blk_iota < own
        gate = jnp.where(candidate, gate, -jnp.inf)
        chosen = jnp.zeros((nb, blk), dtype=jnp.bool_)
        for _ in range(MOBA_TOPK):
            best = _col_max(gate)
            first = jnp.min(jnp.where(gate == best, blk_iota, nb), axis=0, keepdims=True)
            pick = blk_iota == first
            chosen = jnp.logical_or(chosen, pick)
            gate = jnp.where(pick, -jnp.inf, gate)
        chosen = jnp.logical_or(jnp.logical_and(chosen, candidate), blk_iota == own)
        bias = jnp.concatenate([jnp.where(chosen, 0.0, NEG_BIG),
                                jnp.zeros((LANES - nb, blk), _F32)], axis=0)
        qz_ref[h] = qz
        bias_ref[h] = bias.T
        m_ref[h] = jnp.full((1, blk), -jnp.inf, _F32)
        acc_ref[h] = jnp.zeros((VT_HEAD_ROWS, blk), _F32)

    last = HEADS_PER_STEP - 1

    def score_head(g, h):
        s = _nt_dot(keys_with_block_id(h // HEADS_PER_PAIR, g), queries_with_mask(g, h))
        s_ref[h] = s
        smax_ref[h] = _col_max(s)

    def attend_head(g, h, s, s_max):
        m = m_ref[h]
        m_new = jnp.maximum(m, s_max)
        p = jnp.exp2((s - m_new).astype(_BF16))
        acc_ref[h] = jnp.exp2(m - m_new) * acc_ref[h] + _dot(v_rows(h, g), p)
        m_ref[h] = m_new

    k_copy(0, 0).wait()
    for h in range(last):
        score_head(0, h)

    def body(g, carry):
        v_copy(g, g % 2).wait()
        k_copy(g + 1, (g + 1) % K_SLOTS).wait()
        v_copy(g + 1, (g + 1) % 2).start()
        k_copy(g + 2, (g + 2) % K_SLOTS).start()
        score_head(g, last)
        for h in range(HEADS_PER_STEP):
            attend_head(g, h, s_ref[h], smax_ref[h])
            if h < last:
                score_head(g + 1, h)
        return carry

    lax.fori_loop(0, n_full, body, 0)

    score_head(n_full, last)
    v_copy(n_full, n_full % 2).wait()
    k_copy(n_full + 1, (n_full + 1) % K_SLOTS).wait()
    own_in_group = own - n_full * group
    causal_rows = pl.ds(pl.multiple_of((group - 1 - own_in_group) * blk, blk), span)
    for h in range(HEADS_PER_STEP):
        s = s_ref[h] + causal_ref[causal_rows, :]
        attend_head(n_full, h, s, _col_max(s))
    out_t = jnp.concatenate(
        [acc_ref[h, :HEAD_DIM, :] / acc_ref[h, HEAD_DIM:HEAD_DIM + 1, :]
         for h in range(HEADS_PER_STEP)], axis=0)
    o_ref[...] = out_t.T.astype(o_ref.dtype)


def _moba_call(q, k, vt, kmean):
    S = q.shape[0]
    nb = S // MOBA_BLOCK
    assert nb <= LANES
    span = MOBA_GROUP * MOBA_BLOCK
    hot = (jnp.arange(span, dtype=jnp.int32)[:, None] // MOBA_BLOCK
           == jnp.arange(LANES, dtype=jnp.int32)[None, :]).astype(_BF16)
    rel = jnp.arange((2 * MOBA_GROUP - 1) * MOBA_BLOCK, dtype=jnp.int32) - (MOBA_GROUP - 1) * MOBA_BLOCK
    causal = jnp.where(rel[:, None] <= jnp.arange(MOBA_BLOCK, dtype=jnp.int32)[None, :],
                       0.0, NEG_BIG).astype(_F32)
    step_rows = HEADS_PER_STEP * VT_HEAD_ROWS
    once = pl.Buffered(1)
    return pl.pallas_call(
        _moba_kernel,
        grid=(nb,),
        in_specs=[
            pl.BlockSpec((MOBA_BLOCK, STEP_LANES), lambda i: (i, 0)),
            pl.BlockSpec(memory_space=pl.ANY),
            pl.BlockSpec(hot.shape, lambda i: (0, 0), pipeline_mode=once),
            pl.BlockSpec(causal.shape, lambda i: (0, 0), pipeline_mode=once),
            pl.BlockSpec(memory_space=pl.ANY),
            pl.BlockSpec((nb, STEP_LANES), lambda i: (0, 0), pipeline_mode=once),
        ],
        out_specs=pl.BlockSpec((MOBA_BLOCK, STEP_LANES), lambda i: (i, 0)),
        out_shape=jax.ShapeDtypeStruct((S, D_MODEL), _BF16),
        scratch_shapes=[
            pltpu.VMEM((HEADS_PER_STEP, MOBA_BLOCK, LANES), _BF16),
            pltpu.VMEM((HEADS_PER_STEP, MOBA_BLOCK, LANES), _F32),
            pltpu.VMEM((HEADS_PER_STEP, MOBA_GROUP * MOBA_BLOCK, MOBA_BLOCK), _F32),
            pltpu.VMEM((HEADS_PER_STEP, 1, MOBA_BLOCK), _F32),
            pltpu.VMEM((HEADS_PER_STEP, 1, MOBA_BLOCK), _F32),
            pltpu.VMEM((HEADS_PER_STEP, VT_HEAD_ROWS, MOBA_BLOCK), _F32),
            pltpu.VMEM((K_SLOTS, span, D_MODEL), _BF16),
            pltpu.VMEM((2, step_rows, span), _BF16),
            pltpu.SemaphoreType.DMA((K_SLOTS,)),
            pltpu.SemaphoreType.DMA((2,)),
        ],
        compiler_params=pltpu.CompilerParams(
            dimension_semantics=("arbitrary",), vmem_limit_bytes=VMEM_LIMIT),
        name="moba_attention",
    )(q, k, hot, causal, vt, kmean)


def _out_kernel(x1_ref, yatt_ref, yg_ref, sga_ref, wao_ref, wo_ref, g2_ref, b2_ref,
                wg_ref, wu_ref, wd_ref, g3_ref, b3_ref, o_ref):
    y_att = _dot(yatt_ref[...], wao_ref[...])
    merged = (yg_ref[...] + sga_ref[...] * y_att).astype(_BF16)
    z = DEEPNORM_ALPHA * x1_ref[...] + _dot(merged, wo_ref[...])
    x2 = _layer_norm(z, g2_ref[...], b2_ref[...])
    o_ref[...] = _ffn_ln(x2, wg_ref, wu_ref, wd_ref, g3_ref[...], b3_ref[...])


def _out_call(x1, yatt, yg, sga, wao, wo, g2, b2, wg, wu, wd, g3, b3):
    S = x1.shape[0]
    row = pl.BlockSpec((FFN_ROWS, D_MODEL), lambda i: (i, 0))
    weights = (wao, wo, g2, b2, wg, wu, wd, g3, b3)
    return pl.pallas_call(
        _out_kernel,
        grid=(S // FFN_ROWS,),
        in_specs=[row, row, row, row] + [_resident(w.shape) for w in weights],
        out_specs=row,
        out_shape=jax.ShapeDtypeStruct((S, D_MODEL), _F32),
        compiler_params=pltpu.CompilerParams(
            dimension_semantics=("arbitrary",), vmem_limit_bytes=VMEM_LIMIT),
        name="merge_ffn2",
    )(x1, yatt, yg, sga, *weights)


def kernel(x, ffn1_w_gate, ffn1_w_up, ffn1_w_down, ln1_g, ln1_b, w_in, gm_ln_g, gm_ln_b, gm_w_s, gm_b_s, w_gm_out, w_att_out, w_o, ln2_g, ln2_b, ffn2_w_gate, ffn2_w_up, ffn2_w_down, ln3_g, ln3_b):
    B, S, D = x.shape
    assert D == D_MODEL and S % FFN_ROWS == 0 and S % (MOBA_BLOCK * MOBA_GROUP) == 0
    bf = lambda w: w.astype(_BF16)
    outs = []
    for b in range(B):
        xb = x[b]
        for l in range(DEPTH):
            x1 = _ffn1_call(xb, bf(ffn1_w_gate[l]), bf(ffn1_w_up[l]), bf(ffn1_w_down[l]),
                            ln1_g[l][None], ln1_b[l][None])
            bs_full = jnp.repeat(gm_b_s[l].T, GM_GROUP_DIM, axis=1)
            yg, q, k, vt, kmean, sga = _proj_call(
                x1, bf(w_in[l]), gm_ln_g[l][None], gm_ln_b[l][None], gm_w_s[l], bs_full,
                bf(w_gm_out[l]))
            yatt = _moba_call(q, k, vt, kmean.reshape(S // MOBA_BLOCK, D_MODEL))
            xb = _out_call(x1, yatt, yg, sga, bf(w_att_out[l]), bf(w_o[l]),
                           ln2_g[l][None], ln2_b[l][None],
                           bf(ffn2_w_gate[l]), bf(ffn2_w_up[l]), bf(ffn2_w_down[l]),
                           ln3_g[l][None], ln3_b[l][None])
        outs.append(xb)
    return jnp.stack(outs, axis=0)
```

```python
import functools

import jax
import jax.numpy as jnp
import numpy as np
from jax import lax
from jax.experimental import pallas as pl
from jax.experimental.pallas import tpu as pltpu

D_MODEL = 1024
DEPTH = 1
GM_GROUPS = 8
GM_CHUNK = 128
GM_GROUP_DIM = D_MODEL // GM_GROUPS
N_HEADS = 16
HEAD_DIM = 64
MOBA_BLOCK = 256
MOBA_TOPK = 3
ROPE_THETA = 500000.0
ROT_DIM = HEAD_DIM // 4
D_FF = 2816
LN_EPS = 1e-5
DEEPNORM_ALPHA = (2.0 * DEPTH) ** 0.25
ATT_SCALE = HEAD_DIM ** -0.5
LOG2_E = 1.4426950408889634
Q_SCALE = ATT_SCALE * LOG2_E

LANES = 128
FFN_ROWS = 512
FFN_CHUNKS = ((0, 1024), (1024, 1024), (2048, 768))
PROJ_ROWS = 256
HEADS_PER_PAIR = LANES // HEAD_DIM
PAIRS_PER_STEP = D_MODEL // LANES
HEADS_PER_STEP = HEADS_PER_PAIR * PAIRS_PER_STEP
STEP_LANES = PAIRS_PER_STEP * LANES
MOBA_GROUP = 4
K_SLOTS = 3
BF16_SUBLANES = 16
VT_HEAD_ROWS = HEAD_DIM + BF16_SUBLANES
VMEM_LIMIT = 56 * 1024 * 1024
WEIGHT_STAGE_BYTES = 2 * 1024 * 1024
NEG_BIG = -1e30

_BF16 = jnp.bfloat16
_F32 = jnp.float32


def _resident(shape):
    return pl.BlockSpec(shape, lambda *_: (0,) * len(shape), pipeline_mode=pl.Buffered(1))


def _stage_rows(*shapes):
    width = shapes[0][1]
    assert all(shape[1] == width for shape in shapes)
    rows = WEIGHT_STAGE_BYTES // (width * 4) // BF16_SUBLANES * BF16_SUBLANES
    while any(shape[0] % rows for shape in shapes):
        rows -= BF16_SUBLANES
    return rows


def _weight_scratch(*groups):
    copies = [pltpu.VMEM(shape, _BF16) for shapes in groups for shape in shapes]
    stages = [pltpu.VMEM((2, _stage_rows(*shapes), shapes[0][1]), _F32) for shapes in groups]
    return copies + stages + [pltpu.SemaphoreType.DMA((2,))] * len(groups)


def _load_as_bf16(w_hbm, dst_ref, stage_ref, sem):
    rows = stage_ref.shape[1]
    n_chunks = w_hbm.shape[0] // rows

    def chunk(c):
        return pltpu.make_async_copy(w_hbm.at[pl.ds(c * rows, rows), :], stage_ref.at[c % 2], sem.at[c % 2])

    chunk(0).start()
    for c in range(n_chunks):
        if c + 1 < n_chunks:
            chunk(c + 1).start()
        chunk(c).wait()
        dst_ref[pl.ds(c * rows, rows), :] = stage_ref[c % 2].astype(_BF16)


_HBM = pl.BlockSpec(memory_space=pl.ANY)


def _convert_specs(shape, n_steps):
    rows = -(-shape[0] // n_steps)
    rows = -(-rows // BF16_SUBLANES) * BF16_SUBLANES
    while shape[0] % rows:
        rows += BF16_SUBLANES
    last = shape[0] // rows - 1
    spec = pl.BlockSpec((rows, shape[1]), lambda i: (jnp.minimum(i, last), 0))
    return spec, jax.ShapeDtypeStruct(shape, _BF16)


def _layer_norm(z, g, b):
    mu = jnp.mean(z, axis=-1, keepdims=True)
    zc = z - mu
    var = jnp.mean(zc * zc, axis=-1, keepdims=True)
    return zc * lax.rsqrt(var + LN_EPS) * g + b


def _dot(a, b):
    return jnp.dot(a, b, preferred_element_type=_F32)


def _ffn_ln(x, wg_ref, wu_ref, wd_ref, g, b):
    xb = x.astype(_BF16)
    y = None
    for start, width in FFN_CHUNKS:
        gate = _dot(xb, wg_ref[:, start:start + width])
        up = _dot(xb, wu_ref[:, start:start + width])
        h = (jax.nn.silu(gate) * up).astype(_BF16)
        part = _dot(h, wd_ref[start:start + width, :])
        y = part if y is None else y + part
    return _layer_norm(DEEPNORM_ALPHA * x + 0.5 * y, g, b)


def _ffn1_kernel(x_ref, wg_hbm, wu_hbm, wd_hbm, g_ref, b_ref, win_f32, wgo_f32,
                 o_ref, win_b16, wgo_b16,
                 wg_ref, wu_ref, wd_ref, stage_ff, stage_dm, sem_ff, sem_dm):
    win_b16[...] = win_f32[...].astype(_BF16)
    wgo_b16[...] = wgo_f32[...].astype(_BF16)

    @pl.when(pl.program_id(0) == 0)
    def _():
        _load_as_bf16(wg_hbm, wg_ref, stage_ff, sem_ff)
        _load_as_bf16(wu_hbm, wu_ref, stage_ff, sem_ff)
        _load_as_bf16(wd_hbm, wd_ref, stage_dm, sem_dm)

    o_ref[...] = _ffn_ln(x_ref[...], wg_ref, wu_ref, wd_ref, g_ref[...], b_ref[...])


def _ffn1_call(x, wg, wu, wd, g, b, next_weights):
    S = x.shape[0]
    n_steps = S // FFN_ROWS
    row = pl.BlockSpec((FFN_ROWS, D_MODEL), lambda i: (i, 0))
    conv = [_convert_specs(w.shape, n_steps) for w in next_weights]
    outs = pl.pallas_call(
        _ffn1_kernel,
        grid=(n_steps,),
        in_specs=[row, _HBM, _HBM, _HBM, _resident(g.shape), _resident(b.shape)] + [c[0] for c in conv],
        out_specs=[row] + [c[0] for c in conv],
        out_shape=[jax.ShapeDtypeStruct((S, D_MODEL), _F32)] + [c[1] for c in conv],
        scratch_shapes=_weight_scratch([wg.shape, wu.shape], [wd.shape]),
        compiler_params=pltpu.CompilerParams(
            dimension_semantics=("arbitrary",), vmem_limit_bytes=VMEM_LIMIT),
        name="ffn1_ln1",
    )(x, wg, wu, wd, g, b, *next_weights)
    return outs[0], outs[1:]


def _rotary(t, cos_t, sin_lo, sin_hi):
    half = ROT_DIM // 2
    cols = []
    for j in range(D_MODEL // LANES):
        tj = t[:, j * LANES:(j + 1) * LANES]
        fwd = pltpu.roll(tj, LANES - half, axis=1)
        bwd = pltpu.roll(tj, half, axis=1)
        cols.append(tj * cos_t + fwd * sin_lo + bwd * sin_hi)
    return jnp.concatenate(cols, axis=1)


def _proj_kernel(x_ref, win_ref, lng_ref, lnb_ref, ws_ref, bs_ref, wgo_ref,
                 cos_ref, slo_ref, shi_ref, wao_f32, wo_f32, wg2_f32, wu2_f32, wd2_f32,
                 yg_ref, q_ref, k_ref, vt_ref, kmean_ref, sga_ref,
                 wao_b16, wo_b16, wg2_b16, wu2_b16, wd2_b16):
    for src, dst in ((wao_f32, wao_b16), (wo_f32, wo_b16), (wg2_f32, wg2_b16), (wu2_f32, wu2_b16),
                     (wd2_f32, wd2_b16)):
        dst[...] = src[...].astype(_BF16)

    xb = x_ref[...].astype(_BF16)

    def proj(idx):
        return _dot(xb, win_ref[:, idx * D_MODEL:(idx + 1) * D_MODEL])

    u = jax.nn.gelu(proj(0))
    vn = _layer_norm(jax.nn.gelu(proj(1)), lng_ref[...], lnb_ref[...]).astype(_BF16)
    tri = (lax.broadcasted_iota(jnp.int32, (GM_CHUNK, GM_CHUNK), 1)
           <= lax.broadcasted_iota(jnp.int32, (GM_CHUNK, GM_CHUNK), 0))
    ws = [jnp.where(tri, ws_ref[g], 0.0).astype(_BF16) for g in range(GM_GROUPS)]
    rows = []
    for c in range(PROJ_ROWS // GM_CHUNK):
        r0 = c * GM_CHUNK
        groups = [_dot(ws[g], vn[r0:r0 + GM_CHUNK, g * GM_GROUP_DIM:(g + 1) * GM_GROUP_DIM])
                  for g in range(GM_GROUPS)]
        rows.append(jnp.concatenate(groups, axis=1) + bs_ref[...])
    sv = jnp.concatenate(rows, axis=0)
    y_gm = _dot((u * sv).astype(_BF16), wgo_ref[...])
    yg_ref[...] = jax.nn.sigmoid(proj(5)) * y_gm
    sga_ref[...] = jax.nn.sigmoid(proj(6))

    cos_t, sin_lo, sin_hi = cos_ref[...], slo_ref[...], shi_ref[...]
    q = _rotary(proj(2), cos_t, sin_lo, sin_hi)
    q_ref[...] = (q * Q_SCALE).astype(_BF16)
    k = _rotary(proj(3), cos_t, sin_lo, sin_hi)
    k_ref[...] = k.astype(_BF16)
    kmean_ref[0] = jnp.mean(k, axis=0, keepdims=True)
    v_t = proj(4).T.astype(_BF16)
    ones = jnp.ones((VT_HEAD_ROWS - HEAD_DIM, PROJ_ROWS), _BF16)
    for h in range(N_HEADS):
        vt_ref[h * VT_HEAD_ROWS:h * VT_HEAD_ROWS + HEAD_DIM, :] = v_t[h * HEAD_DIM:(h + 1) * HEAD_DIM, :]
        vt_ref[h * VT_HEAD_ROWS + HEAD_DIM:(h + 1) * VT_HEAD_ROWS, :] = ones


def _rope_tables(S):
    half = ROT_DIM // 2
    inv_freq = ROPE_THETA ** (-np.arange(0, ROT_DIM, 2, dtype=np.float64) / ROT_DIM)
    ang = np.arange(S, dtype=np.float64)[:, None] * inv_freq[None, :]
    d = np.arange(LANES) % HEAD_DIM
    pair = d[None, :] % half == np.arange(half)[:, None]
    e_cos = (pair & (d < ROT_DIM)[None, :]).astype(np.float32)
    e_lo = -(pair & (d < half)[None, :]).astype(np.float32)
    e_hi = (pair & ((d >= half) & (d < ROT_DIM))[None, :]).astype(np.float32)

    def spread(values, lanes):
        return jnp.dot(jnp.asarray(values, _F32), jnp.asarray(lanes), precision=lax.Precision.HIGHEST)

    cos_t = spread(np.cos(ang), e_cos) + jnp.asarray((d >= ROT_DIM).astype(np.float32))[None, :]
    return cos_t, spread(np.sin(ang), e_lo), spread(np.sin(ang), e_hi)


def _proj_call(x1, win, lng, lnb, ws, bs_full, wgo, next_weights):
    S = x1.shape[0]
    conv = [_convert_specs(w.shape, S // PROJ_ROWS) for w in next_weights]
    nb = S // MOBA_BLOCK
    cos_t, sin_lo, sin_hi = _rope_tables(S)
    row = pl.BlockSpec((PROJ_ROWS, D_MODEL), lambda i: (i, 0))
    tab = pl.BlockSpec((PROJ_ROWS, LANES), lambda i: (i, 0))
    out_shape = (
        jax.ShapeDtypeStruct((S, D_MODEL), _F32),
        jax.ShapeDtypeStruct((S, D_MODEL), _BF16),
        jax.ShapeDtypeStruct((S, D_MODEL), _BF16),
        jax.ShapeDtypeStruct((N_HEADS * VT_HEAD_ROWS, S), _BF16),
        jax.ShapeDtypeStruct((nb, 1, D_MODEL), _F32),
        jax.ShapeDtypeStruct((S, D_MODEL), _F32),
    )
    out_specs = (
        row, row, row,
        pl.BlockSpec((N_HEADS * VT_HEAD_ROWS, PROJ_ROWS), lambda i: (0, i)),
        pl.BlockSpec((1, 1, D_MODEL), lambda i: (i, 0, 0)),
        row,
    )
    outs = pl.pallas_call(
        _proj_kernel,
        grid=(S // PROJ_ROWS,),
        in_specs=[row, _resident(win.shape), _resident(lng.shape), _resident(lnb.shape),
                  _resident(ws.shape), _resident(bs_full.shape), _resident(wgo.shape),
                  tab, tab, tab] + [c[0] for c in conv],
        out_specs=list(out_specs) + [c[0] for c in conv],
        out_shape=list(out_shape) + [c[1] for c in conv],
        compiler_params=pltpu.CompilerParams(
            dimension_semantics=("arbitrary",), vmem_limit_bytes=VMEM_LIMIT),
        name="inproj_gmlp",
    )(x1, win, lng, lnb, ws, bs_full, wgo, cos_t, sin_lo, sin_hi, *next_weights)
    return outs[:6], outs[6:]


def _nt_dot(a, b):
    return lax.dot_general(a, b, (((1,), (1,)), ((), ())), preferred_element_type=_F32)


def _col_max(s):
    return jnp.max(s, axis=0, keepdims=True)


def _moba_kernel(q_ref, k_hbm, hot_ref, causal_ref, vt_hbm, kmean_ref, o_ref,
                 qz_ref, bias_ref, s_ref, smax_ref, m_ref, acc_ref, k_buf, v_buf, k_sem, v_sem):
    own = pl.program_id(0)
    nb = kmean_ref.shape[0]
    blk, group = MOBA_BLOCK, MOBA_GROUP
    span = group * blk
    n_full = own // group

    blk_iota = lax.broadcasted_iota(jnp.int32, (nb, blk), 0)
    lane_head = lax.broadcasted_iota(jnp.int32, (blk, LANES), 1) // HEAD_DIM

    def pair_lanes(pair):
        return slice(pair * LANES, (pair + 1) * LANES)

    last_group = nb // group - 1

    def k_copy(g, slot):
        start = pl.multiple_of(jnp.minimum(g, last_group) * span, span)
        return pltpu.make_async_copy(k_hbm.at[pl.ds(start, span), :], k_buf.at[slot], k_sem.at[slot])

    def v_copy(g, slot):
        start = pl.multiple_of(jnp.minimum(g, last_group) * span, span)
        return pltpu.make_async_copy(vt_hbm.at[:, pl.ds(start, span)], v_buf.at[slot], v_sem.at[slot])

    k_copy(0, 0).start()
    v_copy(0, 0).start()
    k_copy(1, 1).start()

    def keys_with_block_id(pair, g):
        return jnp.concatenate([k_buf[g % K_SLOTS, :, pair_lanes(pair)], hot_ref[...]], axis=1)

    def queries_with_mask(g, h):
        mask = pltpu.roll(bias_ref[h], (LANES - group * g) % LANES, axis=1)
        return jnp.concatenate([qz_ref[h], mask.astype(_BF16)], axis=1)

    def v_rows(h, g):
        return v_buf[g % 2, h * VT_HEAD_ROWS:(h + 1) * VT_HEAD_ROWS, :]

    kmean = kmean_ref[...]
    km_hi = kmean.astype(_BF16)
    km_rest = kmean - km_hi.astype(_F32)
    km_mid = km_rest.astype(_BF16)
    km_lo = (km_rest - km_mid.astype(_F32)).astype(_BF16)
    for h in range(HEADS_PER_STEP):
        pair = h // HEADS_PER_PAIR
        q_pair = q_ref[:, pair_lanes(pair)]
        km_terms = jnp.concatenate([t[:, pair_lanes(pair)] for t in (km_hi, km_mid, km_lo)],
                                   axis=1)
        qz = jnp.where(lane_head == h % HEADS_PER_PAIR, q_pair, jnp.zeros_like(q_pair))

        gate = _nt_dot(km_terms, jnp.concatenate([qz, qz, qz], axis=1))
        candidate = blk_iota < own
        gate = jnp.where(candidate, gate, -jnp.inf)
        chosen = jnp.zeros((nb, blk), dtype=jnp.bool_)
        for _ in range(MOBA_TOPK):
            best = _col_max(gate)
            first = jnp.min(jnp.where(gate == best, blk_iota, nb), axis=0, keepdims=True)
            pick = blk_iota == first
            chosen = jnp.logical_or(chosen, pick)
            gate = jnp.where(pick, -jnp.inf, gate)
        chosen = jnp.logical_or(jnp.logical_and(chosen, candidate), blk_iota == own)
        bias = jnp.concatenate([jnp.where(chosen, 0.0, NEG_BIG),
                                jnp.zeros((LANES - nb, blk), _F32)], axis=0)
        qz_ref[h] = qz
        bias_ref[h] = bias.T
        m_ref[h] = jnp.full((1, blk), -jnp.inf, _F32)
        acc_ref[h] = jnp.zeros((VT_HEAD_ROWS, blk), _F32)

    last = HEADS_PER_STEP - 1

    def score_head(g, h):
        s = _nt_dot(keys_with_block_id(h // HEADS_PER_PAIR, g), queries_with_mask(g, h))
        s_ref[h] = s
        smax_ref[h] = _col_max(s)

    def attend_head(g, h, s, s_max):
        m = m_ref[h]
        m_new = jnp.maximum(m, s_max)
        p = jnp.exp2((s - m_new).astype(_BF16))
        acc_ref[h] = jnp.exp2(m - m_new) * acc_ref[h] + _dot(v_rows(h, g), p)
        m_ref[h] = m_new

    k_copy(0, 0).wait()
    for h in range(last):
        score_head(0, h)

    def body(g, carry):
        v_copy(g, g % 2).wait()
        k_copy(g + 1, (g + 1) % K_SLOTS).wait()
        v_copy(g + 1, (g + 1) % 2).start()
        k_copy(g + 2, (g + 2) % K_SLOTS).start()
        score_head(g, last)
        for h in range(HEADS_PER_STEP):
            attend_head(g, h, s_ref[h], smax_ref[h])
            if h < last:
                score_head(g + 1, h)
        return carry

    lax.fori_loop(0, n_full, body, 0)

    score_head(n_full, last)
    v_copy(n_full, n_full % 2).wait()
    k_copy(n_full + 1, (n_full + 1) % K_SLOTS).wait()
    own_in_group = own - n_full * group
    causal_rows = pl.ds(pl.multiple_of((group - 1 - own_in_group) * blk, blk), span)
    for h in range(HEADS_PER_STEP):
        s = s_ref[h] + causal_ref[causal_rows, :]
        attend_head(n_full, h, s, _col_max(s))
    out_t = jnp.concatenate(
        [acc_ref[h, :HEAD_DIM, :] / acc_ref[h, HEAD_DIM:HEAD_DIM + 1, :]
         for h in range(HEADS_PER_STEP)], axis=0)
    o_ref[...] = out_t.T.astype(o_ref.dtype)


def _moba_call(q, k, vt, kmean):
    S = q.shape[0]
    nb = S // MOBA_BLOCK
    assert nb <= LANES
    span = MOBA_GROUP * MOBA_BLOCK
    hot = (jnp.arange(span, dtype=jnp.int32)[:, None] // MOBA_BLOCK
           == jnp.arange(LANES, dtype=jnp.int32)[None, :]).astype(_BF16)
    rel = jnp.arange((2 * MOBA_GROUP - 1) * MOBA_BLOCK, dtype=jnp.int32) - (MOBA_GROUP - 1) * MOBA_BLOCK
    causal = jnp.where(rel[:, None] <= jnp.arange(MOBA_BLOCK, dtype=jnp.int32)[None, :],
                       0.0, NEG_BIG).astype(_F32)
    step_rows = HEADS_PER_STEP * VT_HEAD_ROWS
    once = pl.Buffered(1)
    return pl.pallas_call(
        _moba_kernel,
        grid=(nb,),
        in_specs=[
            pl.BlockSpec((MOBA_BLOCK, STEP_LANES), lambda i: (i, 0)),
            pl.BlockSpec(memory_space=pl.ANY),
            pl.BlockSpec(hot.shape, lambda i: (0, 0), pipeline_mode=once),
            pl.BlockSpec(causal.shape, lambda i: (0, 0), pipeline_mode=once),
            pl.BlockSpec(memory_space=pl.ANY),
            pl.BlockSpec((nb, STEP_LANES), lambda i: (0, 0), pipeline_mode=once),
        ],
        out_specs=pl.BlockSpec((MOBA_BLOCK, STEP_LANES), lambda i: (i, 0)),
        out_shape=jax.ShapeDtypeStruct((S, D_MODEL), _BF16),
        scratch_shapes=[
            pltpu.VMEM((HEADS_PER_STEP, MOBA_BLOCK, LANES), _BF16),
            pltpu.VMEM((HEADS_PER_STEP, MOBA_BLOCK, LANES), _F32),
            pltpu.VMEM((HEADS_PER_STEP, MOBA_GROUP * MOBA_BLOCK, MOBA_BLOCK), _F32),
            pltpu.VMEM((HEADS_PER_STEP, 1, MOBA_BLOCK), _F32),
            pltpu.VMEM((HEADS_PER_STEP, 1, MOBA_BLOCK), _F32),
            pltpu.VMEM((HEADS_PER_STEP, VT_HEAD_ROWS, MOBA_BLOCK), _F32),
            pltpu.VMEM((K_SLOTS, span, D_MODEL), _BF16),
            pltpu.VMEM((2, step_rows, span), _BF16),
            pltpu.SemaphoreType.DMA((K_SLOTS,)),
            pltpu.SemaphoreType.DMA((2,)),
        ],
        compiler_params=pltpu.CompilerParams(
            dimension_semantics=("arbitrary",), vmem_limit_bytes=VMEM_LIMIT),
        name="moba_attention",
    )(q, k, hot, causal, vt, kmean)


def _out_kernel(x1_ref, yatt_ref, yg_ref, sga_ref, wao_ref, wo_ref, g2_ref, b2_ref,
                wg_ref, wu_ref, wd_ref, g3_ref, b3_ref, o_ref):
    y_att = _dot(yatt_ref[...], wao_ref[...])
    merged = (yg_ref[...] + sga_ref[...] * y_att).astype(_BF16)
    z = DEEPNORM_ALPHA * x1_ref[...] + _dot(merged, wo_ref[...])
    x2 = _layer_norm(z, g2_ref[...], b2_ref[...])
    o_ref[...] = _ffn_ln(x2, wg_ref, wu_ref, wd_ref, g3_ref[...], b3_ref[...])


def _out_call(x1, yatt, yg, sga, wao, wo, g2, b2, wg, wu, wd, g3, b3):
    S = x1.shape[0]
    row = pl.BlockSpec((FFN_ROWS, D_MODEL), lambda i: (i, 0))
    weights = (wao, wo, g2, b2, wg, wu, wd, g3, b3)
    return pl.pallas_call(
        _out_kernel,
        grid=(S // FFN_ROWS,),
        in_specs=[row, row, row, row] + [_resident(w.shape) for w in weights],
        out_specs=row,
        out_shape=jax.ShapeDtypeStruct((S, D_MODEL), _F32),
        compiler_params=pltpu.CompilerParams(
            dimension_semantics=("arbitrary",), vmem_limit_bytes=VMEM_LIMIT),
        name="merge_ffn2",
    )(x1, yatt, yg, sga, *weights)


def kernel(x, ffn1_w_gate, ffn1_w_up, ffn1_w_down, ln1_g, ln1_b, w_in, gm_ln_g, gm_ln_b, gm_w_s, gm_b_s, w_gm_out, w_att_out, w_o, ln2_g, ln2_b, ffn2_w_gate, ffn2_w_up, ffn2_w_down, ln3_g, ln3_b):
    B, S, D = x.shape
    assert D == D_MODEL and S % FFN_ROWS == 0 and S % (MOBA_BLOCK * MOBA_GROUP) == 0
    outs = []
    for b in range(B):
        xb = x[b]
        for l in range(DEPTH):
            x1, (win_b, wgo_b) = _ffn1_call(xb, ffn1_w_gate[l], ffn1_w_up[l], ffn1_w_down[l],
                                            ln1_g[l][None], ln1_b[l][None], (w_in[l], w_gm_out[l]))
            bs_full = jnp.repeat(gm_b_s[l].T, GM_GROUP_DIM, axis=1)
            (yg, q, k, vt, kmean, sga), tail_weights = _proj_call(
                x1, win_b, gm_ln_g[l][None], gm_ln_b[l][None], gm_w_s[l], bs_full, wgo_b,
                (w_att_out[l], w_o[l], ffn2_w_gate[l], ffn2_w_up[l], ffn2_w_down[l]))
            wao_b, wo_b, wg2_b, wu2_b, wd2_b = tail_weights
            yatt = _moba_call(q, k, vt, kmean.reshape(S // MOBA_BLOCK, D_MODEL))
            xb = _out_call(x1, yatt, yg, sga, wao_b, wo_b,
                           ln2_g[l][None], ln2_b[l][None],
                           wg2_b, wu2_b, wd2_b,
                           ln3_g[l][None], ln3_b[l][None])
        outs.append(xb)
    return jnp.stack(outs, axis=0)
```

```python
import functools

import jax
import jax.numpy as jnp
import numpy as np
from jax import lax
from jax.experimental import pallas as pl
from jax.experimental.pallas import tpu as pltpu

D_MODEL = 1024
DEPTH = 1
GM_GROUPS = 8
GM_CHUNK = 128
GM_GROUP_DIM = D_MODEL // GM_GROUPS
N_HEADS = 16
HEAD_DIM = 64
MOBA_BLOCK = 256
MOBA_TOPK = 3
ROPE_THETA = 500000.0
ROT_DIM = HEAD_DIM // 4
D_FF = 2816
LN_EPS = 1e-5
DEEPNORM_ALPHA = (2.0 * DEPTH) ** 0.25
ATT_SCALE = HEAD_DIM ** -0.5
LOG2_E = 1.4426950408889634
Q_SCALE = ATT_SCALE * LOG2_E

LANES = 128
FFN_ROWS = 512
FFN_CHUNKS = ((0, 1024), (1024, 1024), (2048, 768))
PROJ_ROWS = 256
HEADS_PER_PAIR = LANES // HEAD_DIM
PAIRS_PER_STEP = D_MODEL // LANES
HEADS_PER_STEP = HEADS_PER_PAIR * PAIRS_PER_STEP
STEP_LANES = PAIRS_PER_STEP * LANES
MOBA_GROUP = 4
K_SLOTS = 3
BF16_SUBLANES = 16
VT_HEAD_ROWS = HEAD_DIM
VMEM_LIMIT = 56 * 1024 * 1024
WEIGHT_STAGE_BYTES = 2 * 1024 * 1024
NEG_BIG = -1e30

_BF16 = jnp.bfloat16
_F32 = jnp.float32


def _resident(shape):
    return pl.BlockSpec(shape, lambda *_: (0,) * len(shape), pipeline_mode=pl.Buffered(1))


def _stage_rows(*shapes):
    width = shapes[0][1]
    assert all(shape[1] == width for shape in shapes)
    rows = WEIGHT_STAGE_BYTES // (width * 4) // BF16_SUBLANES * BF16_SUBLANES
    while any(shape[0] % rows for shape in shapes):
        rows -= BF16_SUBLANES
    return rows


def _weight_scratch(*groups):
    copies = [pltpu.VMEM(shape, _BF16) for shapes in groups for shape in shapes]
    stages = [pltpu.VMEM((2, _stage_rows(*shapes), shapes[0][1]), _F32) for shapes in groups]
    return copies + stages + [pltpu.SemaphoreType.DMA((2,))] * len(groups)


def _load_as_bf16(w_hbm, dst_ref, stage_ref, sem):
    rows = stage_ref.shape[1]
    n_chunks = w_hbm.shape[0] // rows

    def chunk(c):
        return pltpu.make_async_copy(w_hbm.at[pl.ds(c * rows, rows), :], stage_ref.at[c % 2], sem.at[c % 2])

    chunk(0).start()
    for c in range(n_chunks):
        if c + 1 < n_chunks:
            chunk(c + 1).start()
        chunk(c).wait()
        dst_ref[pl.ds(c * rows, rows), :] = stage_ref[c % 2].astype(_BF16)


_HBM = pl.BlockSpec(memory_space=pl.ANY)


def _convert_specs(shape, n_steps):
    rows = -(-shape[0] // n_steps)
    rows = -(-rows // BF16_SUBLANES) * BF16_SUBLANES
    while shape[0] % rows:
        rows += BF16_SUBLANES
    last = shape[0] // rows - 1
    spec = pl.BlockSpec((rows, shape[1]), lambda i: (jnp.minimum(i, last), 0))
    return spec, jax.ShapeDtypeStruct(shape, _BF16)


def _layer_norm(z, g, b):
    mu = jnp.mean(z, axis=-1, keepdims=True)
    zc = z - mu
    var = jnp.mean(zc * zc, axis=-1, keepdims=True)
    return zc * lax.rsqrt(var + LN_EPS) * g + b


def _dot(a, b):
    return jnp.dot(a, b, preferred_element_type=_F32)


def _ffn_ln(x, wg_ref, wu_ref, wd_ref, g, b):
    xb = x.astype(_BF16)
    y = None
    for start, width in FFN_CHUNKS:
        gate = _dot(xb, wg_ref[:, start:start + width])
        up = _dot(xb, wu_ref[:, start:start + width])
        h = (jax.nn.silu(gate) * up).astype(_BF16)
        part = _dot(h, wd_ref[start:start + width, :])
        y = part if y is None else y + part
    return _layer_norm(DEEPNORM_ALPHA * x + 0.5 * y, g, b)


def _ffn1_kernel(x_ref, wg_hbm, wu_hbm, wd_hbm, g_ref, b_ref, win_f32, wgo_f32,
                 o_ref, win_b16, wgo_b16,
                 wg_ref, wu_ref, wd_ref, stage_ff, stage_dm, sem_ff, sem_dm):
    win_b16[...] = win_f32[...].astype(_BF16)
    wgo_b16[...] = wgo_f32[...].astype(_BF16)

    @pl.when(pl.program_id(0) == 0)
    def _():
        _load_as_bf16(wg_hbm, wg_ref, stage_ff, sem_ff)
        _load_as_bf16(wu_hbm, wu_ref, stage_ff, sem_ff)
        _load_as_bf16(wd_hbm, wd_ref, stage_dm, sem_dm)

    o_ref[...] = _ffn_ln(x_ref[...], wg_ref, wu_ref, wd_ref, g_ref[...], b_ref[...])


def _ffn1_call(x, wg, wu, wd, g, b, next_weights):
    S = x.shape[0]
    n_steps = S // FFN_ROWS
    row = pl.BlockSpec((FFN_ROWS, D_MODEL), lambda i: (i, 0))
    conv = [_convert_specs(w.shape, n_steps) for w in next_weights]
    outs = pl.pallas_call(
        _ffn1_kernel,
        grid=(n_steps,),
        in_specs=[row, _HBM, _HBM, _HBM, _resident(g.shape), _resident(b.shape)] + [c[0] for c in conv],
        out_specs=[row] + [c[0] for c in conv],
        out_shape=[jax.ShapeDtypeStruct((S, D_MODEL), _F32)] + [c[1] for c in conv],
        scratch_shapes=_weight_scratch([wg.shape, wu.shape], [wd.shape]),
        compiler_params=pltpu.CompilerParams(
            dimension_semantics=("arbitrary",), vmem_limit_bytes=VMEM_LIMIT),
        name="ffn1_ln1",
    )(x, wg, wu, wd, g, b, *next_weights)
    return outs[0], outs[1:]


def _rotary(t, cos_t, sin_lo, sin_hi):
    half = ROT_DIM // 2
    cols = []
    for j in range(D_MODEL // LANES):
        tj = t[:, j * LANES:(j + 1) * LANES]
        fwd = pltpu.roll(tj, LANES - half, axis=1)
        bwd = pltpu.roll(tj, half, axis=1)
        cols.append(tj * cos_t + fwd * sin_lo + bwd * sin_hi)
    return jnp.concatenate(cols, axis=1)


def _proj_kernel(x_ref, win_ref, lng_ref, lnb_ref, ws_ref, bs_ref, wgo_ref,
                 cos_ref, slo_ref, shi_ref, wao_f32, wo_f32, wg2_f32, wu2_f32, wd2_f32,
                 yg_ref, q_ref, k_ref, vt_ref, kmean_ref, sga_ref,
                 wao_b16, wo_b16, wg2_b16, wu2_b16, wd2_b16):
    for src, dst in ((wao_f32, wao_b16), (wo_f32, wo_b16), (wg2_f32, wg2_b16), (wu2_f32, wu2_b16),
                     (wd2_f32, wd2_b16)):
        dst[...] = src[...].astype(_BF16)

    xb = x_ref[...].astype(_BF16)

    def proj(idx):
        return _dot(xb, win_ref[:, idx * D_MODEL:(idx + 1) * D_MODEL])

    u = jax.nn.gelu(proj(0))
    vn = _layer_norm(jax.nn.gelu(proj(1)), lng_ref[...], lnb_ref[...]).astype(_BF16)
    tri = (lax.broadcasted_iota(jnp.int32, (GM_CHUNK, GM_CHUNK), 1)
           <= lax.broadcasted_iota(jnp.int32, (GM_CHUNK, GM_CHUNK), 0))
    ws = [jnp.where(tri, ws_ref[g], 0.0).astype(_BF16) for g in range(GM_GROUPS)]
    rows = []
    for c in range(PROJ_ROWS // GM_CHUNK):
        r0 = c * GM_CHUNK
        groups = [_dot(ws[g], vn[r0:r0 + GM_CHUNK, g * GM_GROUP_DIM:(g + 1) * GM_GROUP_DIM])
                  for g in range(GM_GROUPS)]
        rows.append(jnp.concatenate(groups, axis=1) + bs_ref[...])
    sv = jnp.concatenate(rows, axis=0)
    y_gm = _dot((u * sv).astype(_BF16), wgo_ref[...])
    yg_ref[...] = jax.nn.sigmoid(proj(5)) * y_gm
    sga_ref[...] = jax.nn.sigmoid(proj(6))

    cos_t, sin_lo, sin_hi = cos_ref[...], slo_ref[...], shi_ref[...]
    q = _rotary(proj(2), cos_t, sin_lo, sin_hi)
    q_ref[...] = (q * Q_SCALE).astype(_BF16)
    k = _rotary(proj(3), cos_t, sin_lo, sin_hi)
    k_ref[...] = k.astype(_BF16)
    kmean_ref[0] = jnp.mean(k, axis=0, keepdims=True)
    vt_ref[...] = proj(4).T.astype(_BF16)


def _rope_tables(S):
    half = ROT_DIM // 2
    inv_freq = ROPE_THETA ** (-np.arange(0, ROT_DIM, 2, dtype=np.float64) / ROT_DIM)
    ang = np.arange(S, dtype=np.float64)[:, None] * inv_freq[None, :]
    d = np.arange(LANES) % HEAD_DIM
    pair = d[None, :] % half == np.arange(half)[:, None]
    e_cos = (pair & (d < ROT_DIM)[None, :]).astype(np.float32)
    e_lo = -(pair & (d < half)[None, :]).astype(np.float32)
    e_hi = (pair & ((d >= half) & (d < ROT_DIM))[None, :]).astype(np.float32)

    def spread(values, lanes):
        return jnp.dot(jnp.asarray(values, _F32), jnp.asarray(lanes), precision=lax.Precision.HIGHEST)

    cos_t = spread(np.cos(ang), e_cos) + jnp.asarray((d >= ROT_DIM).astype(np.float32))[None, :]
    return cos_t, spread(np.sin(ang), e_lo), spread(np.sin(ang), e_hi)


def _proj_call(x1, win, lng, lnb, ws, bs_full, wgo, next_weights):
    S = x1.shape[0]
    conv = [_convert_specs(w.shape, S // PROJ_ROWS) for w in next_weights]
    nb = S // MOBA_BLOCK
    cos_t, sin_lo, sin_hi = _rope_tables(S)
    row = pl.BlockSpec((PROJ_ROWS, D_MODEL), lambda i: (i, 0))
    tab = pl.BlockSpec((PROJ_ROWS, LANES), lambda i: (i, 0))
    out_shape = (
        jax.ShapeDtypeStruct((S, D_MODEL), _F32),
        jax.ShapeDtypeStruct((S, D_MODEL), _BF16),
        jax.ShapeDtypeStruct((S, D_MODEL), _BF16),
        jax.ShapeDtypeStruct((N_HEADS * VT_HEAD_ROWS, S), _BF16),
        jax.ShapeDtypeStruct((nb, 1, D_MODEL), _F32),
        jax.ShapeDtypeStruct((S, D_MODEL), _F32),
    )
    out_specs = (
        row, row, row,
        pl.BlockSpec((N_HEADS * VT_HEAD_ROWS, PROJ_ROWS), lambda i: (0, i)),
        pl.BlockSpec((1, 1, D_MODEL), lambda i: (i, 0, 0)),
        row,
    )
    outs = pl.pallas_call(
        _proj_kernel,
        grid=(S // PROJ_ROWS,),
        in_specs=[row, _resident(win.shape), _resident(lng.shape), _resident(lnb.shape),
                  _resident(ws.shape), _resident(bs_full.shape), _resident(wgo.shape),
                  tab, tab, tab] + [c[0] for c in conv],
        out_specs=list(out_specs) + [c[0] for c in conv],
        out_shape=list(out_shape) + [c[1] for c in conv],
        compiler_params=pltpu.CompilerParams(
            dimension_semantics=("arbitrary",), vmem_limit_bytes=VMEM_LIMIT),
        name="inproj_gmlp",
    )(x1, win, lng, lnb, ws, bs_full, wgo, cos_t, sin_lo, sin_hi, *next_weights)
    return outs[:6], outs[6:]


def _nt_dot(a, b):
    return lax.dot_general(a, b, (((1,), (1,)), ((), ())), preferred_element_type=_F32)


def _col_max(s):
    return jnp.max(s, axis=0, keepdims=True)


def _moba_kernel(q_ref, k_hbm, hot_ref, causal_ref, vt_hbm, kmean_ref, o_ref,
                 qz_ref, bias_ref, s_ref, smax_ref, m_ref, acc_ref, l_ref, k_buf, v_buf, k_sem, v_sem):
    own = pl.program_id(0)
    nb = kmean_ref.shape[0]
    blk, group = MOBA_BLOCK, MOBA_GROUP
    span = group * blk
    n_full = own // group

    blk_iota = lax.broadcasted_iota(jnp.int32, (nb, blk), 0)
    lane_head = lax.broadcasted_iota(jnp.int32, (blk, LANES), 1) // HEAD_DIM

    def pair_lanes(pair):
        return slice(pair * LANES, (pair + 1) * LANES)

    last_group = nb // group - 1

    def k_copy(g, slot):
        start = pl.multiple_of(jnp.minimum(g, last_group) * span, span)
        return pltpu.make_async_copy(k_hbm.at[pl.ds(start, span), :], k_buf.at[slot], k_sem.at[slot])

    def v_copy(g, slot):
        start = pl.multiple_of(jnp.minimum(g, last_group) * span, span)
        return pltpu.make_async_copy(vt_hbm.at[:, pl.ds(start, span)], v_buf.at[slot], v_sem.at[slot])

    k_copy(0, 0).start()
    v_copy(0, 0).start()
    k_copy(1, 1).start()

    def keys_with_block_id(pair, g):
        return jnp.concatenate([k_buf[g % K_SLOTS, :, pair_lanes(pair)], hot_ref[...]], axis=1)

    def queries_with_mask(g, h):
        mask = pltpu.roll(bias_ref[h], (LANES - group * g) % LANES, axis=1)
        return jnp.concatenate([qz_ref[h], mask.astype(_BF16)], axis=1)

    def v_rows(h, g):
        return v_buf[g % 2, h * VT_HEAD_ROWS:(h + 1) * VT_HEAD_ROWS, :]

    kmean = kmean_ref[...]
    km_hi = kmean.astype(_BF16)
    km_rest = kmean - km_hi.astype(_F32)
    km_mid = km_rest.astype(_BF16)
    km_lo = (km_rest - km_mid.astype(_F32)).astype(_BF16)
    for h in range(HEADS_PER_STEP):
        pair = h // HEADS_PER_PAIR
        q_pair = q_ref[:, pair_lanes(pair)]
        km_terms = jnp.concatenate([t[:, pair_lanes(pair)] for t in (km_hi, km_mid, km_lo)],
                                   axis=1)
        qz = jnp.where(lane_head == h % HEADS_PER_PAIR, q_pair, jnp.zeros_like(q_pair))

        gate = _nt_dot(km_terms, jnp.concatenate([qz, qz, qz], axis=1))
        candidate = blk_iota < own
        gate = jnp.where(candidate, gate, -jnp.inf)
        chosen = jnp.zeros((nb, blk), dtype=jnp.bool_)
        for _ in range(MOBA_TOPK):
            best = _col_max(gate)
            first = jnp.min(jnp.where(gate == best, blk_iota, nb), axis=0, keepdims=True)
            pick = blk_iota == first
            chosen = jnp.logical_or(chosen, pick)
            gate = jnp.where(pick, -jnp.inf, gate)
        chosen = jnp.logical_or(jnp.logical_and(chosen, candidate), blk_iota == own)
        bias = jnp.concatenate([jnp.where(chosen, 0.0, NEG_BIG),
                                jnp.zeros((LANES - nb, blk), _F32)], axis=0)
        qz_ref[h] = qz
        bias_ref[h] = bias.T
        m_ref[h] = jnp.full((1, blk), -jnp.inf, _F32)
        acc_ref[h] = jnp.zeros((VT_HEAD_ROWS, blk), _F32)
        l_ref[h] = jnp.zeros((1, blk), _F32)

    last = HEADS_PER_STEP - 1

    def score_head(g, h):
        s = _nt_dot(keys_with_block_id(h // HEADS_PER_PAIR, g), queries_with_mask(g, h))
        s_ref[h] = s
        smax_ref[h] = _col_max(s)

    def attend_head(g, h, s, s_max):
        m = m_ref[h]
        m_new = jnp.maximum(m, s_max)
        p = jnp.exp2((s - m_new).astype(_BF16))
        alpha = jnp.exp2(m - m_new)
        l_ref[h] = alpha * l_ref[h] + jnp.sum(p.astype(_F32), axis=0, keepdims=True)
        acc_ref[h] = alpha * acc_ref[h] + _dot(v_rows(h, g), p)
        m_ref[h] = m_new

    k_copy(0, 0).wait()
    for h in range(last):
        score_head(0, h)

    def body(g, carry):
        v_copy(g, g % 2).wait()
        k_copy(g + 1, (g + 1) % K_SLOTS).wait()
        v_copy(g + 1, (g + 1) % 2).start()
        k_copy(g + 2, (g + 2) % K_SLOTS).start()
        score_head(g, last)
        for h in range(HEADS_PER_STEP):
            attend_head(g, h, s_ref[h], smax_ref[h])
            if h < last:
                score_head(g + 1, h)
        return carry

    lax.fori_loop(0, n_full, body, 0)

    score_head(n_full, last)
    v_copy(n_full, n_full % 2).wait()
    k_copy(n_full + 1, (n_full + 1) % K_SLOTS).wait()
    own_in_group = own - n_full * group
    causal_rows = pl.ds(pl.multiple_of((group - 1 - own_in_group) * blk, blk), span)
    for h in range(HEADS_PER_STEP):
        s = s_ref[h] + causal_ref[causal_rows, :]
        attend_head(n_full, h, s, _col_max(s))
    out_t = jnp.concatenate(
        [acc_ref[h] / l_ref[h]
         for h in range(HEADS_PER_STEP)], axis=0)
    o_ref[...] = out_t.T.astype(o_ref.dtype)


def _moba_call(q, k, vt, kmean):
    S = q.shape[0]
    nb = S // MOBA_BLOCK
    assert nb <= LANES
    span = MOBA_GROUP * MOBA_BLOCK
    hot = (jnp.arange(span, dtype=jnp.int32)[:, None] // MOBA_BLOCK
           == jnp.arange(LANES, dtype=jnp.int32)[None, :]).astype(_BF16)
    rel = jnp.arange((2 * MOBA_GROUP - 1) * MOBA_BLOCK, dtype=jnp.int32) - (MOBA_GROUP - 1) * MOBA_BLOCK
    causal = jnp.where(rel[:, None] <= jnp.arange(MOBA_BLOCK, dtype=jnp.int32)[None, :],
                       0.0, NEG_BIG).astype(_F32)
    step_rows = HEADS_PER_STEP * VT_HEAD_ROWS
    once = pl.Buffered(1)
    return pl.pallas_call(
        _moba_kernel,
        grid=(nb,),
        in_specs=[
            pl.BlockSpec((MOBA_BLOCK, STEP_LANES), lambda i: (i, 0)),
            pl.BlockSpec(memory_space=pl.ANY),
            pl.BlockSpec(hot.shape, lambda i: (0, 0), pipeline_mode=once),
            pl.BlockSpec(causal.shape, lambda i: (0, 0), pipeline_mode=once),
            pl.BlockSpec(memory_space=pl.ANY),
            pl.BlockSpec((nb, STEP_LANES), lambda i: (0, 0), pipeline_mode=once),
        ],
        out_specs=pl.BlockSpec((MOBA_BLOCK, STEP_LANES), lambda i: (i, 0)),
        out_shape=jax.ShapeDtypeStruct((S, D_MODEL), _BF16),
        scratch_shapes=[
            pltpu.VMEM((HEADS_PER_STEP, MOBA_BLOCK, LANES), _BF16),
            pltpu.VMEM((HEADS_PER_STEP, MOBA_BLOCK, LANES), _F32),
            pltpu.VMEM((HEADS_PER_STEP, MOBA_GROUP * MOBA_BLOCK, MOBA_BLOCK), _F32),
            pltpu.VMEM((HEADS_PER_STEP, 1, MOBA_BLOCK), _F32),
            pltpu.VMEM((HEADS_PER_STEP, 1, MOBA_BLOCK), _F32),
            pltpu.VMEM((HEADS_PER_STEP, VT_HEAD_ROWS, MOBA_BLOCK), _F32),
            pltpu.VMEM((HEADS_PER_STEP, 1, MOBA_BLOCK), _F32),
            pltpu.VMEM((K_SLOTS, span, D_MODEL), _BF16),
            pltpu.VMEM((2, step_rows, span), _BF16),
            pltpu.SemaphoreType.DMA((K_SLOTS,)),
            pltpu.SemaphoreType.DMA((2,)),
        ],
        compiler_params=pltpu.CompilerParams(
            dimension_semantics=("arbitrary",), vmem_limit_bytes=VMEM_LIMIT),
        name="moba_attention",
    )(q, k, hot, causal, vt, kmean)


def _out_kernel(x1_ref, yatt_ref, yg_ref, sga_ref, wao_ref, wo_ref, g2_ref, b2_ref,
                wg_ref, wu_ref, wd_ref, g3_ref, b3_ref, o_ref):
    y_att = _dot(yatt_ref[...], wao_ref[...])
    merged = (yg_ref[...] + sga_ref[...] * y_att).astype(_BF16)
    z = DEEPNORM_ALPHA * x1_ref[...] + _dot(merged, wo_ref[...])
    x2 = _layer_norm(z, g2_ref[...], b2_ref[...])
    o_ref[...] = _ffn_ln(x2, wg_ref, wu_ref, wd_ref, g3_ref[...], b3_ref[...])


def _out_call(x1, yatt, yg, sga, wao, wo, g2, b2, wg, wu, wd, g3, b3):
    S = x1.shape[0]
    row = pl.BlockSpec((FFN_ROWS, D_MODEL), lambda i: (i, 0))
    weights = (wao, wo, g2, b2, wg, wu, wd, g3, b3)
    return pl.pallas_call(
        _out_kernel,
        grid=(S // FFN_ROWS,),
        in_specs=[row, row, row, row] + [_resident(w.shape) for w in weights],
        out_specs=row,
        out_shape=jax.ShapeDtypeStruct((S, D_MODEL), _F32),
        compiler_params=pltpu.CompilerParams(
            dimension_semantics=("arbitrary",), vmem_limit_bytes=VMEM_LIMIT),
        name="merge_ffn2",
    )(x1, yatt, yg, sga, *weights)


def kernel(x, ffn1_w_gate, ffn1_w_up, ffn1_w_down, ln1_g, ln1_b, w_in, gm_ln_g, gm_ln_b, gm_w_s, gm_b_s, w_gm_out, w_att_out, w_o, ln2_g, ln2_b, ffn2_w_gate, ffn2_w_up, ffn2_w_down, ln3_g, ln3_b):
    B, S, D = x.shape
    assert D == D_MODEL and S % FFN_ROWS == 0 and S % (MOBA_BLOCK * MOBA_GROUP) == 0
    outs = []
    for b in range(B):
        xb = x[b]
        for l in range(DEPTH):
            x1, (win_b, wgo_b) = _ffn1_call(xb, ffn1_w_gate[l], ffn1_w_up[l], ffn1_w_down[l],
                                            ln1_g[l][None], ln1_b[l][None], (w_in[l], w_gm_out[l]))
            bs_full = jnp.repeat(gm_b_s[l].T, GM_GROUP_DIM, axis=1)
            (yg, q, k, vt, kmean, sga), tail_weights = _proj_call(
                x1, win_b, gm_ln_g[l][None], gm_ln_b[l][None], gm_w_s[l], bs_full, wgo_b,
                (w_att_out[l], w_o[l], ffn2_w_gate[l], ffn2_w_up[l], ffn2_w_down[l]))
            wao_b, wo_b, wg2_b, wu2_b, wd2_b = tail_weights
            yatt = _moba_call(q, k, vt, kmean.reshape(S // MOBA_BLOCK, D_MODEL))
            xb = _out_call(x1, yatt, yg, sga, wao_b, wo_b,
                           ln2_g[l][None], ln2_b[l][None],
                           wg2_b, wu2_b, wd2_b,
                           ln3_g[l][None], ln3_b[l][None])
        outs.append(xb)
    return jnp.stack(outs, axis=0)
```
